```python
import math
import jax
import jax.numpy as jnp
from jax import lax
import numpy as np

D_MODEL = 2048
BATCH = 1
SEQ = 8192
DEPTH = 4

N_BRANCHES = 3
BRANCH_WIDTH = D_MODEL // 2
D_FF = 256 * math.ceil(8 * D_MODEL / 3 / 256)
FFN_RESIDUAL_SCALE = 0.5
NORM_EPS = 1e-6

SB_HEAD_DIM = 64
SB_HEADS = BRANCH_WIDTH // SB_HEAD_DIM
SB_BLOCK = 128

GLA_HEADS = 4
GLA_DV = BRANCH_WIDTH // GLA_HEADS
GLA_DK = GLA_DV // 2
GLA_GATE_RANK = 16
GLA_TAU = 16.0
GLA_CHUNK = 64

RWKV_HEAD_DIM = 64
RWKV_HEADS = BRANCH_WIDTH // RWKV_HEAD_DIM
RWKV_DECAY_RANK = max(32, int(round(1.8 * BRANCH_WIDTH ** 0.5 / 32)) * 32)
RWKV_A_RANK = max(32, int(round(1.8 * BRANCH_WIDTH ** 0.5 / 32)) * 32)
RWKV_GATE_RANK = max(32, int(round(0.6 * BRANCH_WIDTH ** 0.8 / 32)) * 32)
RWKV_LN_EPS = 64e-5

SB_COLS = (BRANCH_WIDTH, BRANCH_WIDTH, BRANCH_WIDTH)
GLA_COLS = (GLA_HEADS * GLA_DK, GLA_HEADS * GLA_DK, BRANCH_WIDTH, BRANCH_WIDTH, GLA_GATE_RANK)
RWKV_COLS = (BRANCH_WIDTH, BRANCH_WIDTH, BRANCH_WIDTH, RWKV_DECAY_RANK, RWKV_A_RANK, RWKV_GATE_RANK)
GROUP_COLS = (sum(SB_COLS), sum(GLA_COLS), sum(RWKV_COLS), N_BRANCHES * D_MODEL)
IN_COLS = sum(GROUP_COLS)

kernel_name = 'hybrid_sb_gla_rwkv7_macaron'


def rms_norm(x, gain):
    xf = x.astype(jnp.float32)
    y = xf * lax.rsqrt(jnp.mean(xf * xf, axis=-1, keepdims=True) + NORM_EPS)
    return (y * gain.astype(jnp.float32)).astype(x.dtype)


def split_cols(y, sizes):
    idx, acc = [], 0
    for s in sizes[:-1]:
        acc += s
        idx.append(acc)
    return jnp.split(y, idx, axis=-1)


def to_heads(t, n_heads):
    return t.reshape(t.shape[0], t.shape[1], n_heads, -1)


def swiglu(h, w_in, w_out):
    gate, up = jnp.split(h @ w_in, 2, axis=-1)
    return (jax.nn.silu(gate) * up) @ w_out


def stick_breaking_attention(q, k, v):
    B, S, H, Dh = q.shape
    scale = Dh ** -0.5
    qh = jnp.transpose(q, (0, 2, 1, 3)).astype(jnp.float32)
    kh = jnp.transpose(k, (0, 2, 1, 3)).astype(jnp.float32)
    vh = jnp.transpose(v, (0, 2, 1, 3)).astype(jnp.float32)
    n_blocks = S // SB_BLOCK
    q_blocks = qh.reshape(B, H, n_blocks, SB_BLOCK, Dh).transpose(2, 0, 1, 3, 4)
    k_pos = jnp.arange(S)

    def one_block(args):
        blk, qb = args
        z = jnp.einsum('bhqd,bhkd->bhqk', qb, kh) * scale
        q_pos = blk * SB_BLOCK + jnp.arange(SB_BLOCK)
        causal = k_pos[None, :] < q_pos[:, None]
        log_keep = jnp.where(causal, jax.nn.log_sigmoid(-z), 0.0)
        log_later = lax.cumsum(log_keep, axis=3, reverse=True) - log_keep
        weights = jnp.where(causal, jnp.exp(jax.nn.log_sigmoid(z) + log_later), 0.0)
        return jnp.einsum('bhqk,bhkd->bhqd', weights, vh)

    out = lax.map(one_block, (jnp.arange(n_blocks), q_blocks))
    return out.transpose(1, 0, 3, 2, 4).reshape(B, S, H, Dh)


def gla_attention(q, k, v, log_alpha):
    B, S, H, DK = q.shape
    DV = v.shape[-1]
    C = GLA_CHUNK
    N = S // C

    def chunks(t):
        return t.reshape(B, N, C, H, t.shape[-1]).transpose(0, 3, 1, 2, 4)

    qc, kc, vc, gc = chunks(q * DK ** -0.5), chunks(k), chunks(v), chunks(log_alpha)
    b = jnp.cumsum(gc, axis=3)
    b_last = b[:, :, :, -1:, :]
    q_dec = qc * jnp.exp(b)
    k_dec = kc * jnp.exp(-b)
    causal = jnp.tril(jnp.ones((C, C), dtype=bool))
    scores = jnp.where(causal, jnp.einsum('bhnid,bhnjd->bhnij', q_dec, k_dec), 0.0)
    o_intra = jnp.einsum('bhnij,bhnje->bhnie', scores, vc)
    chunk_kv = jnp.einsum('bhnjd,bhnje->bhnde', kc * jnp.exp(b_last - b), vc)
    chunk_decay = jnp.exp(b_last[:, :, :, 0, :])

    def step(state, inp):
        decay, kv = inp
        return decay[..., None] * state + kv, state

    init = jnp.zeros((B, H, DK, DV), jnp.float32)
    _, prev_states = lax.scan(step, init, (jnp.moveaxis(chunk_decay, 2, 0), jnp.moveaxis(chunk_kv, 2, 0)))
    prev_states = jnp.moveaxis(prev_states, 0, 2)
    o_inter = jnp.einsum('bhnid,bhnde->bhnie', q_dec, prev_states)
    return (o_intra + o_inter).transpose(0, 2, 3, 1, 4).reshape(B, S, H, DV)


def rwkv7_recurrence(r, w, k, v, kk, a):
    B, S, H, N = r.shape

    def step(state, inp):
        r_t, w_t, k_t, v_t, kk_t, a_t = inp
        removed = jnp.einsum('bhvk,bhk->bhv', state, kk_t)
        state = (state * w_t[:, :, None, :]
                 - removed[..., None] * (kk_t * a_t)[:, :, None, :]
                 + v_t[..., None] * k_t[:, :, None, :])
        return state, jnp.einsum('bhvk,bhk->bhv', state, r_t)

    xs = tuple(jnp.moveaxis(t, 1, 0) for t in (r, w, k, v, kk, a))
    _, out = lax.scan(step, jnp.zeros((B, H, N, N), jnp.float32), xs)
    return jnp.moveaxis(out, 0, 1)


def token_mixing(h, w_in, gla_gate_up, gla_gate_bias, gla_norm, rwkv_mu, rwkv_w_up, rwkv_w0,
                 rwkv_a_up, rwkv_a0, rwkv_g_up, rwkv_k_k, rwkv_k_a, rwkv_r_k, rwkv_ln_w, rwkv_ln_b,
                 w_branch, w_out):
    B, S, _ = h.shape
    f32 = jnp.float32
    y = h @ w_in
    y_sb, y_gla, y_rwkv, gate_logits = split_cols(y, GROUP_COLS)

    q_a, k_a, v_a = split_cols(y_sb, SB_COLS)
    o_a = stick_breaking_attention(to_heads(q_a, SB_HEADS), to_heads(k_a, SB_HEADS), to_heads(v_a, SB_HEADS))
    o_a = o_a.reshape(B, S, BRANCH_WIDTH)

    q_b, k_b, v_b, r_b, alpha_low = split_cols(y_gla, GLA_COLS)
    log_alpha = jax.nn.log_sigmoid((alpha_low @ gla_gate_up + gla_gate_bias).astype(f32)) / GLA_TAU
    o_b = gla_attention(to_heads(q_b.astype(f32), GLA_HEADS), to_heads(k_b.astype(f32), GLA_HEADS),
                        to_heads(v_b.astype(f32), GLA_HEADS), to_heads(log_alpha, GLA_HEADS))
    o_b = o_b * lax.rsqrt(jnp.mean(o_b * o_b, axis=-1, keepdims=True) + NORM_EPS)
    o_b = o_b.reshape(B, S, BRANCH_WIDTH) * gla_norm * jax.nn.silu(r_b.astype(f32))

    y_prev = jnp.pad(y_rwkv, ((0, 0), (1, 0), (0, 0)))[:, :-1]
    y_rwkv = y_rwkv + (y_prev - y_rwkv) * rwkv_mu
    r_c, k_c, v_c, w_low, a_low, g_low = split_cols(y_rwkv, RWKV_COLS)
    w_log = -jax.nn.softplus(-(rwkv_w0 + jnp.tanh(w_low) @ rwkv_w_up).astype(f32)) - 0.5
    decay = jnp.exp(-jnp.exp(w_log))
    a = jax.nn.sigmoid((rwkv_a0 + a_low @ rwkv_a_up).astype(f32))
    g = (jax.nn.sigmoid(g_low) @ rwkv_g_up).astype(f32)
    k_c = k_c.astype(f32)
    kk = to_heads(k_c * rwkv_k_k, RWKV_HEADS)
    kk = kk / jnp.maximum(jnp.sqrt(jnp.sum(kk * kk, axis=-1, keepdims=True)), 1e-12)
    k_c = k_c * (1.0 + (a - 1.0) * rwkv_k_a)
    r_h = to_heads(r_c.astype(f32), RWKV_HEADS)
    k_h = to_heads(k_c, RWKV_HEADS)
    v_h = to_heads(v_c.astype(f32), RWKV_HEADS)
    wkv = rwkv7_recurrence(r_h, to_heads(decay, RWKV_HEADS), k_h, v_h, kk, to_heads(a, RWKV_HEADS))
    mean = jnp.mean(wkv, axis=-1, keepdims=True)
    var = jnp.var(wkv, axis=-1, keepdims=True)
    wkv = ((wkv - mean) * lax.rsqrt(var + RWKV_LN_EPS)).reshape(B, S, BRANCH_WIDTH) * rwkv_ln_w + rwkv_ln_b
    bonus = (jnp.sum(r_h * k_h * rwkv_r_k, axis=-1, keepdims=True) * v_h).reshape(B, S, BRANCH_WIDTH)
    o_c = (wkv + bonus) * g

    branches = jnp.stack([o_a, o_b, o_c], axis=2).astype(h.dtype)
    u = jnp.einsum('bsgc,gcd->bsgd', branches, w_branch)
    gates = jax.nn.sigmoid(gate_logits.reshape(B, S, N_BRANCHES, D_MODEL))
    merged = jnp.sum(gates * u, axis=2)
    return merged @ w_out


def setup_inputs(seed: int = 0) -> dict:
    key = jax.random.key(seed)
    ks = jax.random.split(key, 22)
    f32 = jnp.float32
    L, D, W = DEPTH, D_MODEL, BRANCH_WIDTH

    def normal(k, shape, scale):
        return jax.random.normal(k, shape, f32) * scale

    return {
        'x': normal(ks[0], (BATCH, SEQ, D), 1.0),
        'norm_pre': 1.0 + normal(ks[1], (L, 3, D), 0.1),
        'norm_post': 1.0 + normal(ks[2], (L, 3, D), 0.1),
        'ffn_in': normal(ks[3], (L, 2, D, 2 * D_FF), D ** -0.5),
        'ffn_out': normal(ks[4], (L, 2, D_FF, D), D_FF ** -0.5),
        'w_in': normal(ks[5], (L, D, IN_COLS), D ** -0.5),
        'gla_gate_up': normal(ks[6], (L, GLA_GATE_RANK, GLA_HEADS * GLA_DK), GLA_GATE_RANK ** -0.5),
        'gla_gate_bias': normal(ks[7], (L, GLA_HEADS * GLA_DK), 0.1),
        'gla_norm': 1.0 + normal(ks[8], (L, W), 0.1),
        'rwkv_mu': jax.random.uniform(ks[9], (L, sum(RWKV_COLS)), f32, 0.0, 1.0),
        'rwkv_w_up': normal(ks[10], (L, RWKV_DECAY_RANK, W), RWKV_DECAY_RANK ** -0.5),
        'rwkv_w0': jax.random.uniform(ks[11], (L, W), f32, -6.0, -1.0),
        'rwkv_a_up': normal(ks[12], (L, RWKV_A_RANK, W), RWKV_A_RANK ** -0.5),
        'rwkv_a0': normal(ks[13], (L, W), 0.1),
        'rwkv_g_up': normal(ks[14], (L, RWKV_GATE_RANK, W), RWKV_GATE_RANK ** -0.5),
        'rwkv_k_k': 0.85 + normal(ks[15], (L, W), 0.05),
        'rwkv_k_a': 1.0 + normal(ks[16], (L, W), 0.05),
        'rwkv_r_k': normal(ks[17], (L, RWKV_HEADS, RWKV_HEAD_DIM), 0.1),
        'rwkv_ln_w': 1.0 + normal(ks[18], (L, W), 0.1),
        'rwkv_ln_b': normal(ks[19], (L, W), 0.02),
        'w_branch': normal(ks[20], (L, N_BRANCHES, W, D), W ** -0.5),
        'w_out': normal(ks[21], (L, D, D), D ** -0.5),
    }


def reference(x, norm_pre, norm_post, ffn_in, ffn_out, w_in, gla_gate_up, gla_gate_bias, gla_norm,
              rwkv_mu, rwkv_w_up, rwkv_w0, rwkv_a_up, rwkv_a0, rwkv_g_up, rwkv_k_k, rwkv_k_a, rwkv_r_k,
              rwkv_ln_w, rwkv_ln_b, w_branch, w_out):
    for l in range(DEPTH):
        h = rms_norm(x, norm_pre[l, 0])
        x = x + FFN_RESIDUAL_SCALE * rms_norm(swiglu(h, ffn_in[l, 0], ffn_out[l, 0]), norm_post[l, 0])
        h = rms_norm(x, norm_pre[l, 1])
        mix = token_mixing(h, w_in[l], gla_gate_up[l], gla_gate_bias[l], gla_norm[l], rwkv_mu[l],
                           rwkv_w_up[l], rwkv_w0[l], rwkv_a_up[l], rwkv_a0[l], rwkv_g_up[l], rwkv_k_k[l],
                           rwkv_k_a[l], rwkv_r_k[l], rwkv_ln_w[l], rwkv_ln_b[l], w_branch[l], w_out[l])
        x = x + rms_norm(mix, norm_post[l, 1])
        h = rms_norm(x, norm_pre[l, 2])
        x = x + FFN_RESIDUAL_SCALE * rms_norm(swiglu(h, ffn_in[l, 1], ffn_out[l, 1]), norm_post[l, 1 + 1])
    return x
```

```python
import functools
import math

import jax
import jax.numpy as jnp
from jax import lax
from jax.experimental import pallas as pl
from jax.experimental.pallas import tpu as pltpu

F32 = jnp.float32
BF16 = jnp.bfloat16

D_MODEL = 2048
BRANCH = D_MODEL // 2
D_FF = 256 * math.ceil(8 * D_MODEL / 3 / 256)
FFN_SCALE = 0.5
NORM_EPS = 1e-6

SB_HEAD_DIM = 64
GLA_HEADS = 4
GLA_DV = BRANCH // GLA_HEADS
GLA_DK = GLA_DV // 2
GLA_RANK = 16
GLA_TAU = 16.0
GLA_CHUNK = 64
RW_N = 64
RW_RANK = 64
RW_GATE_RANK = 160
RW_LN_EPS = 64e-5

LANES = 128
MXU_DIM = 256
VMEM_LIMIT_CAP = 60000 * 1024

C_GLA_Q = 0
C_GLA_K = 512
C_GLA_V = 1024
C_GLA_R = 2048
C_RW_R = 3072
C_RW_K = 4096
C_RW_V = 5120
C_GATES = 6144
C_ALPHA = 12288
C_WA = 12416
C_GLOW = 12544
N_REST = 12800
N_SB = 3 * BRANCH


def _vmem_limit(nbytes):
    return int(min(max(2 * nbytes, 32 * 1024 * 1024), VMEM_LIMIT_CAP))


def _params(sem, nbytes):
    return pltpu.CompilerParams(dimension_semantics=sem, vmem_limit_bytes=_vmem_limit(nbytes))


def _rms(x, gain):
    ms = jnp.mean(x * x, axis=-1, keepdims=True)
    return x * lax.rsqrt(ms + NORM_EPS) * gain


def _softplus(x):
    return jnp.maximum(x, 0.0) + jnp.log(1.0 + jnp.exp(-jnp.abs(x)))


def _split(x, n):
    parts = []
    for _ in range(n - 1):
        p = x.astype(BF16)
        parts.append(p)
        x = x - p.astype(F32)
    parts.append(x.astype(BF16))
    return parts


_NN = (((1,), (0,)), ((), ()))
_NT = (((1,), (1,)), ((), ()))


def _mm(a_parts, b_parts, dims=_NN, order=3):
    acc = None
    for i, a in enumerate(a_parts):
        for j, b in enumerate(b_parts):
            if i + j < order:
                t = lax.dot_general(a, b, dims, preferred_element_type=F32)
                acc = t if acc is None else acc + t
    return acc


def _ffn_kernel(x_ref, gpre_ref, gpost_ref, wg_ref, wu_ref, wo_ref, o_ref, h_ref, acc_ref):
    j = pl.program_id(1)

    @pl.when(j == 0)
    def _():
        h_ref[...] = _rms(x_ref[...], gpre_ref[...]).astype(BF16)
        acc_ref[...] = jnp.zeros_like(acc_ref)

    h = h_ref[...]
    g = jnp.dot(h, wg_ref[...], preferred_element_type=F32)
    u = jnp.dot(h, wu_ref[...], preferred_element_type=F32)
    a = (g * jax.nn.sigmoid(g) * u).astype(BF16)
    acc_ref[...] += jnp.dot(a, wo_ref[...], preferred_element_type=F32)

    @pl.when(j == pl.num_programs(1) - 1)
    def _():
        o_ref[...] = x_ref[...] + FFN_SCALE * _rms(acc_ref[...], gpost_ref[...])


def _ffn(x, gpre, gpost, w_in, w_out, tm=512, tf=512):
    s, d = x.shape
    f = w_out.shape[0]
    tm = min(tm, s)
    nj = f // tf
    nbytes = 2 * (2 * tm * d * 4) + tm * d * 6 + 2 * (3 * d * tf * 2) + 4 * tm * tf * 4
    return pl.pallas_call(
        _ffn_kernel,
        grid=(s // tm, nj),
        in_specs=[
            pl.BlockSpec((tm, d), lambda i, j: (i, 0)),
            pl.BlockSpec((1, d), lambda i, j: (0, 0)),
            pl.BlockSpec((1, d), lambda i, j: (0, 0)),
            pl.BlockSpec((d, tf), lambda i, j: (0, j)),
            pl.BlockSpec((d, tf), lambda i, j: (0, j + nj)),
            pl.BlockSpec((tf, d), lambda i, j: (j, 0)),
        ],
        out_specs=pl.BlockSpec((tm, d), lambda i, j: (i, 0)),
        out_shape=jax.ShapeDtypeStruct((s, d), F32),
        scratch_shapes=[pltpu.VMEM((tm, d), BF16), pltpu.VMEM((tm, d), F32)],
        compiler_params=_params(("parallel", "arbitrary"), nbytes),
        name="ffn",
    )(x, gpre, gpost, w_in, w_in, w_out)


def _norm_matmul_kernel(x_ref, g_ref, w_ref, o_ref, h_ref):
    @pl.when(pl.program_id(1) == 0)
    def _():
        h_ref[...] = _rms(x_ref[...], g_ref[...]).astype(BF16)

    o_ref[...] = jnp.dot(h_ref[...], w_ref[...], preferred_element_type=F32).astype(o_ref.dtype)


def _norm_matmul(x, gain, w, out_dtype, tm=1024, tn=512):
    s, d = x.shape
    n = w.shape[1]
    tm = min(tm, s)
    nbytes = 2 * tm * d * 4 + tm * d * 2 + 2 * d * tn * 2 + 2 * tm * tn * 4
    return pl.pallas_call(
        _norm_matmul_kernel,
        grid=(s // tm, n // tn),
        in_specs=[
            pl.BlockSpec((tm, d), lambda i, j: (i, 0)),
            pl.BlockSpec((1, d), lambda i, j: (0, 0)),
            pl.BlockSpec((d, tn), lambda i, j: (0, j)),
        ],
        out_specs=pl.BlockSpec((tm, tn), lambda i, j: (i, j)),
        out_shape=jax.ShapeDtypeStruct((s, n), out_dtype),
        scratch_shapes=[pltpu.VMEM((tm, d), BF16)],
        compiler_params=_params(("parallel", "arbitrary"), nbytes),
        name="norm_matmul",
    )(x, gain, w)


SB_TILE = 256


def _sb_kernel(q_ref, k_ref, v_ref, o_ref):
    t = SB_TILE
    qi = pl.program_id(1)
    left = lax.broadcasted_iota(jnp.int32, (1, LANES), 1) < SB_HEAD_DIM
    q = q_ref[...]
    qz = jnp.zeros_like(q)
    q_heads = (jnp.where(left, q, qz), jnp.where(left, qz, q))
    row = lax.broadcasted_iota(jnp.int32, (t, t), 0)
    col = lax.broadcasted_iota(jnp.int32, (t, t), 1)
    tri = jnp.where(row >= col, 1.0, 0.0).astype(BF16)
    causal = col < row

    def tile(kb, carry, masked):
        acc, c0, c1 = carry
        start = pl.multiple_of(kb * t, t)
        k = k_ref[pl.ds(start, t), :]
        v = v_ref[pl.ds(start, t), :]
        vz = jnp.zeros_like(v)
        v_heads = (jnp.where(left, v, vz), jnp.where(left, vz, v))
        cs = [c0, c1]
        for hh in range(2):
            z = lax.dot_general(q_heads[hh], k, _NT, preferred_element_type=F32)
            lk = -_softplus(z)
            if masked:
                lk = jnp.where(causal, lk, 0.0)
            cum = _mm(_split(lk, 2), [tri], order=2)
            w = jnp.exp(z + cum + cs[hh])
            if masked:
                w = jnp.where(causal, w, 0.0)
            acc = acc + jnp.dot(w.astype(BF16), v_heads[hh], preferred_element_type=F32)
            cs[hh] = cs[hh] + jnp.sum(lk, axis=-1, keepdims=True)
        return acc, cs[0], cs[1]

    init = (jnp.zeros((t, LANES), F32), jnp.zeros((t, 1), F32), jnp.zeros((t, 1), F32))
    carry = tile(qi, init, True)
    carry = lax.fori_loop(0, qi, lambda j, c: tile(qi - 1 - j, c, False), carry)
    o_ref[...] = carry[0].astype(o_ref.dtype)


def _sb_attention(qkv):
    s = qkv.shape[0]
    t = SB_TILE
    npair = BRANCH // LANES
    nbytes = 2 * 2 * s * LANES * 2 + 16 * t * t * 4
    return pl.pallas_call(
        _sb_kernel,
        grid=(npair, s // t),
        in_specs=[
            pl.BlockSpec((t, LANES), lambda p, i: (i, p)),
            pl.BlockSpec((s, LANES), lambda p, i: (0, npair + p)),
            pl.BlockSpec((s, LANES), lambda p, i: (0, 2 * npair + p)),
        ],
        out_specs=pl.BlockSpec((t, LANES), lambda p, i: (i, p)),
        out_shape=jax.ShapeDtypeStruct((s, BRANCH), BF16),
        compiler_params=_params(("parallel", "arbitrary"), nbytes),
        name="sb_attention",
    )(qkv, qkv, qkv)


GLA_TILE = 256


def _gla_kernel(q_ref, k_ref, v_ref, r_ref, al_ref, gup_ref, gb_ref, gn_ref, o_ref, st_ref):
    t, c = GLA_TILE, GLA_CHUNK

    @pl.when(pl.program_id(1) == 0)
    def _():
        st_ref[...] = jnp.zeros_like(st_ref)

    x = jnp.dot(al_ref[...].astype(BF16), gup_ref[...], preferred_element_type=F32) + gb_ref[...]
    g = -_softplus(-x) * (1.0 / GLA_TAU)
    row = lax.broadcasted_iota(jnp.int32, (t, t), 0)
    col = lax.broadcasted_iota(jnp.int32, (t, t), 1)
    same = (row // c) == (col // c)
    tril = same & (col <= row)
    g_parts = _split(g, 3)
    b = _mm([jnp.where(tril, 1.0, 0.0).astype(BF16)], g_parts)
    b_last = _mm([jnp.where(same, 1.0, 0.0).astype(BF16)], g_parts)
    q = q_ref[...]
    k = k_ref[...]
    v = v_ref[...].astype(BF16)
    q_dec = (q * (GLA_DK ** -0.5) * jnp.exp(b)).astype(BF16)
    k_dec = (k * jnp.exp(-b)).astype(BF16)
    k_tail = (k * jnp.exp(b_last - b)).astype(BF16)
    chunk_decay = jnp.exp(b_last)
    scores = lax.dot_general(q_dec, k_dec, _NT, preferred_element_type=F32)
    scores = jnp.where(tril, scores, 0.0).astype(BF16)
    o = jnp.dot(scores, v, preferred_element_type=F32)
    outs = []
    st = st_ref[...]
    for n in range(t // c):
        rows = slice(n * c, (n + 1) * c)
        o_inter = lax.dot_general(q_dec[rows], st.astype(BF16), _NT, preferred_element_type=F32)
        outs.append(o[rows] + o_inter)
        kv = jnp.dot(v[rows].astype(F32).T.astype(BF16), k_tail[rows], preferred_element_type=F32)
        st = st * chunk_decay[n * c:n * c + 1, :] + kv
    st_ref[...] = st
    o = jnp.concatenate(outs, axis=0)
    o = o * lax.rsqrt(jnp.mean(o * o, axis=-1, keepdims=True) + NORM_EPS)
    r = r_ref[...]
    o_ref[...] = (o * gn_ref[...] * (r * jax.nn.sigmoid(r))).astype(o_ref.dtype)


def _gla(y, gate_up, gate_bias, gla_norm):
    s = y.shape[0]
    t = min(GLA_TILE, s)
    dk, dv = GLA_DK, GLA_DV
    nbytes = 2 * t * (2 * dk + 2 * dv + LANES) * 4 + 12 * t * t * 4
    return pl.pallas_call(
        _gla_kernel,
        grid=(GLA_HEADS, s // t),
        in_specs=[
            pl.BlockSpec((t, dk), lambda h, i: (i, C_GLA_Q // dk + h)),
            pl.BlockSpec((t, dk), lambda h, i: (i, C_GLA_K // dk + h)),
            pl.BlockSpec((t, dv), lambda h, i: (i, C_GLA_V // dv + h)),
            pl.BlockSpec((t, dv), lambda h, i: (i, C_GLA_R // dv + h)),
            pl.BlockSpec((t, LANES), lambda h, i: (i, C_ALPHA // LANES)),
            pl.BlockSpec((LANES, dk), lambda h, i: (0, h)),
            pl.BlockSpec((1, dk), lambda h, i: (0, h)),
            pl.BlockSpec((1, dv), lambda h, i: (0, h)),
        ],
        out_specs=pl.BlockSpec((t, dv), lambda h, i: (i, h)),
        out_shape=jax.ShapeDtypeStruct((s, BRANCH), BF16),
        scratch_shapes=[pltpu.VMEM((dv, dk), F32)],
        compiler_params=_params(("parallel", "arbitrary"), nbytes),
        name="gla",
    )(y, y, y, y, y, gate_up, gate_bias, gla_norm)


def _head_ones():
    r = lax.broadcasted_iota(jnp.int32, (MXU_DIM, MXU_DIM), 0) // RW_N
    c = lax.broadcasted_iota(jnp.int32, (MXU_DIM, MXU_DIM), 1) // RW_N
    return jnp.where(r == c, 1.0, 0.0).astype(BF16)


def _head_sum(x, ones_bd, n_split):
    outs = []
    for i in range(x.shape[1] // MXU_DIM):
        outs.append(_mm(_split(x[:, i * MXU_DIM:(i + 1) * MXU_DIM], n_split), [ones_bd], order=n_split))
    return jnp.concatenate(outs, axis=1)


def _rwkv_prep_kernel(yr_ref, yk_ref, yv_ref, ywa_ref, yg_ref, pr_ref, pk_ref, pv_ref, pwa_ref, pg_ref,
                      mur_ref, muk_ref, muv_ref, muwa_ref, mug_ref, wup_ref, aup_ref, gup_ref,
                      w0_ref, a0_ref, kk_ref, ka_ref, rk_ref,
                      r_out, lw_out, k_out, v_out, kk_out, a_out, g_out, bonus_out):
    first = pl.program_id(0) == 0

    def shifted(y_ref, p_ref, mu_ref):
        y = y_ref[...]
        prev_row = jnp.where(first, 0.0, p_ref[7:8, :])
        row = lax.broadcasted_iota(jnp.int32, y.shape, 0)
        y_prev = jnp.where(row == 0, prev_row, pltpu.roll(y, 1, axis=0))
        return y + (y_prev - y) * mu_ref[...]

    r = shifted(yr_ref, pr_ref, mur_ref)
    k = shifted(yk_ref, pk_ref, muk_ref)
    v = shifted(yv_ref, pv_ref, muv_ref)
    wa = shifted(ywa_ref, pwa_ref, muwa_ref)
    gl = shifted(yg_ref, pg_ref, mug_ref)
    w_pre = w0_ref[...] + jnp.dot(jnp.tanh(wa).astype(BF16), wup_ref[...], preferred_element_type=F32)
    lw = -jnp.exp(-_softplus(-w_pre) - 0.5)
    a = jax.nn.sigmoid(a0_ref[...] + jnp.dot(wa.astype(BF16), aup_ref[...], preferred_element_type=F32))
    g = jnp.dot(jax.nn.sigmoid(gl).astype(BF16), gup_ref[...], preferred_element_type=F32)
    ones_bd = _head_ones()
    kk = k * kk_ref[...]
    kk = kk / jnp.maximum(jnp.sqrt(_head_sum(kk * kk, ones_bd, 3)), 1e-12)
    k = k * (1.0 + (a - 1.0) * ka_ref[...])
    bonus = _head_sum(r * k * rk_ref[...], ones_bd, 3) * v
    r_out[...] = r
    lw_out[...] = lw
    k_out[...] = k
    v_out[...] = v
    kk_out[...] = kk
    a_out[...] = a
    g_out[...] = g
    bonus_out[...] = bonus


def _rwkv_prep(y, mu_r, mu_k, mu_v, mu_wa, mu_g, wup, aup, gup, w0, a0, k_k, k_a, r_k, tm=512):
    s = y.shape[0]
    tm = min(tm, s)
    w = BRANCH

    def cur(width, col):
        return pl.BlockSpec((tm, width), lambda i: (i, col // width))

    def prev(width, col):
        return pl.BlockSpec((8, width), lambda i: (jnp.maximum(i * (tm // 8) - 1, 0), col // width))

    def full(shape):
        return pl.BlockSpec(shape, lambda i: (0, 0))

    nbytes = 2 * tm * (3 * w + 384) * 4 + 2 * 8 * tm * w * 4 + 12 * tm * w * 4
    out = jax.ShapeDtypeStruct((s, w), F32)
    return pl.pallas_call(
        _rwkv_prep_kernel,
        grid=(s // tm,),
        in_specs=[cur(w, C_RW_R), cur(w, C_RW_K), cur(w, C_RW_V), cur(LANES, C_WA), cur(2 * LANES, C_GLOW),
                  prev(w, C_RW_R), prev(w, C_RW_K), prev(w, C_RW_V), prev(LANES, C_WA), prev(2 * LANES, C_GLOW),
                  full((1, w)), full((1, w)), full((1, w)), full((1, LANES)), full((1, 2 * LANES)),
                  full((LANES, w)), full((LANES, w)), full((2 * LANES, w)),
                  full((1, w)), full((1, w)), full((1, w)), full((1, w)), full((1, w))],
        out_specs=[pl.BlockSpec((tm, w), lambda i: (i, 0))] * 8,
        out_shape=[out] * 8,
        compiler_params=_params(("parallel",), nbytes),
        name="rwkv_prep",
    )(y, y, y, y, y, y, y, y, y, y, mu_r, mu_k, mu_v, mu_wa, mu_g, wup, aup, gup, w0, a0, k_k, k_a, r_k)


RW_CHUNK = 64
RW_PACK = MXU_DIM // RW_N


def _rwkv_kernel(r_ref, lw_ref, k_ref, v_ref, kk_ref, a_ref, o_ref, s_ref):
    c, g4 = RW_CHUNK, MXU_DIM

    @pl.when(pl.program_id(0) == 0)
    def _():
        s_ref[...] = jnp.zeros_like(s_ref)

    ti = lax.broadcasted_iota(jnp.int32, (c, g4), 0)
    ci = lax.broadcasted_iota(jnp.int32, (c, g4), 1) % c
    strict = ci < ti
    incl = ci <= ti
    lane_head = lax.broadcasted_iota(jnp.int32, (1, g4), 1) // RW_N
    head_masks = [lane_head == h for h in range(RW_PACK)]
    bd_mask = (lax.broadcasted_iota(jnp.int32, (g4, g4), 0) // RW_N) == (lax.broadcasted_iota(jnp.int32, (g4, g4), 1) // RW_N)
    tr = lax.broadcasted_iota(jnp.int32, (c, c), 0)
    tc = lax.broadcasted_iota(jnp.int32, (c, c), 1)
    tri = jnp.where(tc <= tr, 1.0, 0.0).astype(BF16)

    def stack(parts):
        return [jnp.concatenate([jnp.where(m, p, jnp.zeros_like(p)) for m in head_masks], axis=0) for p in parts]

    lw_all = lw_ref[...]
    cum_all = _mm([tri], _split(lw_all, 3))

    for grp in range(BRANCH // g4):
        sl = slice(grp * g4, (grp + 1) * g4)
        lw = lw_all[:, sl]
        cum = cum_all[:, sl]
        cum_end = cum[c - 1:c, :]
        g_in = jnp.exp(cum)
        g_ex = jnp.exp(cum - lw)
        g_inv = jnp.exp(-cum)
        tail = jnp.exp(cum_end - cum)
        g_end = jnp.exp(cum_end)
        kk = kk_ref[:, sl]
        kka = kk * a_ref[:, sl]
        k = k_ref[:, sl]
        v = v_ref[:, sl]
        alp = kk * g_ex
        rho = r_ref[:, sl] * g_in
        ar = _split(jnp.concatenate([alp, rho], axis=0), 2)
        pb = _mm(ar, stack(_split(kka * g_inv, 2)), _NT)
        pk = _mm(ar, stack(_split(k * g_inv, 2)), _NT)
        l_b = jnp.where(strict, pb[:c], 0.0)
        m_b = jnp.where(incl, pb[c:], 0.0)
        l_k = jnp.where(strict, pk[:c], 0.0)
        m_k = jnp.where(incl, pk[c:], 0.0)

        x = jnp.where(ci == ti, 1.0, 0.0)
        m = 1
        while m < c:
            lower = ((ti // m) % 2 == 1) & ((ci // m) == (ti // m) - 1)
            y = _mm(_split(x, 2), stack(_split(jnp.where(lower, l_b, 0.0), 2)))
            x = x - _mm(_split(y, 2), stack(_split(x, 2)))
            m *= 2

        st = s_ref[grp]
        ars = _mm(ar, _split(st, 2), _NT)
        v_stack = stack(_split(v, 2))
        x0 = ars[:c] + _mm(_split(l_k, 2), v_stack)
        u = _mm(_split(x, 2), stack(_split(x0, 2)))
        o = ars[c:] - _mm(_split(m_b, 2), stack(_split(u, 2))) + _mm(_split(m_k, 2), v_stack)
        vu_t = jnp.concatenate([v, -u], axis=0).T
        kb = jnp.concatenate([k * tail, kka * tail], axis=0)
        st_new = st * g_end + _mm(_split(vu_t, 2), _split(kb, 2))
        s_ref[grp] = jnp.where(bd_mask, st_new, 0.0)
        o_ref[:, sl] = o


def _rwkv_recurrence(r, lw, k, v, kk, a):
    s, w = r.shape
    c = RW_CHUNK
    spec = pl.BlockSpec((c, w), lambda i: (i, 0))
    nbytes = 2 * 7 * c * w * 4 + (w // MXU_DIM) * MXU_DIM * MXU_DIM * 4 + 64 * MXU_DIM * MXU_DIM * 4
    return pl.pallas_call(
        _rwkv_kernel,
        grid=(s // c,),
        in_specs=[spec] * 6,
        out_specs=spec,
        out_shape=jax.ShapeDtypeStruct((s, w), F32),
        scratch_shapes=[pltpu.VMEM((w // MXU_DIM, MXU_DIM, MXU_DIM), F32)],
        compiler_params=_params(("arbitrary",), nbytes),
        name="rwkv_recurrence",
    )(r, lw, k, v, kk, a)


def _rwkv_post_kernel(wkv_ref, g_ref, bonus_ref, lnw_ref, lnb_ref, o_ref):
    ones_bd = _head_ones()
    x = wkv_ref[...]
    d = x - _head_sum(x, ones_bd, 3) * (1.0 / RW_N)
    var = _head_sum(d * d, ones_bd, 3) * (1.0 / RW_N)
    y = d * lax.rsqrt(var + RW_LN_EPS) * lnw_ref[...] + lnb_ref[...]
    o_ref[...] = ((y + bonus_ref[...]) * g_ref[...]).astype(o_ref.dtype)


def _rwkv_post(wkv, g, bonus, ln_w, ln_b, tm=512):
    s, w = wkv.shape
    tm = min(tm, s)
    blk = pl.BlockSpec((tm, w), lambda i: (i, 0))
    vec = pl.BlockSpec((1, w), lambda i: (0, 0))
    nbytes = 2 * 4 * tm * w * 4 + 8 * tm * w * 4
    return pl.pallas_call(
        _rwkv_post_kernel,
        grid=(s // tm,),
        in_specs=[blk, blk, blk, vec, vec],
        out_specs=blk,
        out_shape=jax.ShapeDtypeStruct((s, w), BF16),
        compiler_params=_params(("parallel",), nbytes),
        name="rwkv_post",
    )(wkv, g, bonus, ln_w, ln_b)


def _merge_kernel(oa_ref, ob_ref, oc_ref, wb_ref, ga_ref, gb_ref, gc_ref, o_ref):
    acc = None
    for o_g, gate, idx in ((oa_ref, ga_ref, 0), (ob_ref, gb_ref, 1), (oc_ref, gc_ref, 2)):
        u = jnp.dot(o_g[...], wb_ref[idx], preferred_element_type=F32)
        t = jax.nn.sigmoid(gate[...]) * u
        acc = t if acc is None else acc + t
    o_ref[...] = acc.astype(o_ref.dtype)


def _merge(oa, ob, oc, w_branch, y, tm=512, tn=512):
    s, w = oa.shape
    d = w_branch.shape[2]
    tm = min(tm, s)
    branch = pl.BlockSpec((tm, w), lambda i, j: (i, 0))

    def gate(gi):
        return pl.BlockSpec((tm, tn), lambda i, j: (i, (C_GATES + gi * d) // tn + j))

    nbytes = 2 * 3 * tm * w * 2 + 2 * 3 * w * tn * 2 + 2 * 4 * tm * tn * 4 + 4 * tm * tn * 4
    return pl.pallas_call(
        _merge_kernel,
        grid=(s // tm, d // tn),
        in_specs=[branch, branch, branch, pl.BlockSpec((3, w, tn), lambda i, j: (0, 0, j)),
                  gate(0), gate(1), gate(2)],
        out_specs=pl.BlockSpec((tm, tn), lambda i, j: (i, j)),
        out_shape=jax.ShapeDtypeStruct((s, d), BF16),
        compiler_params=_params(("parallel", "arbitrary"), nbytes),
        name="merge",
    )(oa, ob, oc, w_branch, y, y, y)


def _out_proj_kernel(m_ref, w_ref, x_ref, g_ref, o_ref):
    mix = jnp.dot(m_ref[...], w_ref[...], preferred_element_type=F32)
    o_ref[...] = x_ref[...] + _rms(mix, g_ref[...])


def _out_proj(merged, w_out, x, gain, tm=512):
    s, d = x.shape
    tm = min(tm, s)
    nbytes = 2 * tm * d * 2 + 2 * d * d * 2 + 4 * tm * d * 4 + 2 * tm * d * 4
    return pl.pallas_call(
        _out_proj_kernel,
        grid=(s // tm,),
        in_specs=[pl.BlockSpec((tm, d), lambda i: (i, 0)), pl.BlockSpec((d, d), lambda i: (0, 0)),
                  pl.BlockSpec((tm, d), lambda i: (i, 0)), pl.BlockSpec((1, d), lambda i: (0, 0))],
        out_specs=pl.BlockSpec((tm, d), lambda i: (i, 0)),
        out_shape=jax.ShapeDtypeStruct((s, d), F32),
        compiler_params=_params(("parallel",), nbytes),
        name="out_proj",
    )(merged, w_out, x, gain)


def _pad_cols(w, width):
    return jnp.pad(w, ((0, 0), (0, width - w.shape[1])))


def _pad_rows(w, height, offset=0):
    return jnp.pad(w, ((offset, height - offset - w.shape[0]), (0, 0)))


def _pack_w_in(w):
    b = BRANCH
    gla0 = 3 * b
    alpha0 = gla0 + 2 * GLA_HEADS * GLA_DK + 2 * b
    rw0 = alpha0 + GLA_RANK
    wa0 = rw0 + 3 * b
    gl0 = wa0 + 2 * RW_RANK
    gates0 = gl0 + RW_GATE_RANK
    w_sb = jnp.concatenate([w[:, :b] * (SB_HEAD_DIM ** -0.5), w[:, b:3 * b]], axis=1).astype(BF16)
    w_rest = jnp.concatenate([
        w[:, gla0:alpha0], w[:, rw0:wa0], w[:, gates0:],
        _pad_cols(w[:, alpha0:rw0], LANES), w[:, wa0:gl0], _pad_cols(w[:, gl0:gates0], 2 * LANES)],
        axis=1).astype(BF16)
    return w_sb, w_rest


def _token_mixing(x, gain, w_in, gla_gate_up, gla_gate_bias, gla_norm, rwkv_mu, rwkv_w_up, rwkv_w0,
                  rwkv_a_up, rwkv_a0, rwkv_g_up, rwkv_k_k, rwkv_k_a, rwkv_r_k, rwkv_ln_w, rwkv_ln_b,
                  w_branch, w_out, gain_post):
    b = BRANCH
    w_sb, w_rest = _pack_w_in(w_in)
    qkv = _norm_matmul(x, gain, w_sb, BF16)
    y = _norm_matmul(x, gain, w_rest, F32)

    o_a = _sb_attention(qkv)

    o_b = _gla(y, _pad_rows(gla_gate_up, LANES).astype(BF16), gla_gate_bias[None, :], gla_norm[None, :])

    mu = rwkv_mu
    row = lambda t: t[None, :]
    prep = _rwkv_prep(
        y, row(mu[:b]), row(mu[b:2 * b]), row(mu[2 * b:3 * b]), row(mu[3 * b:3 * b + 2 * RW_RANK]),
        _pad_cols(row(mu[3 * b + 2 * RW_RANK:]), 2 * LANES),
        _pad_rows(rwkv_w_up, LANES).astype(BF16), _pad_rows(rwkv_a_up, LANES, RW_RANK).astype(BF16),
        _pad_rows(rwkv_g_up, 2 * LANES).astype(BF16),
        row(rwkv_w0), row(rwkv_a0), row(rwkv_k_k), row(rwkv_k_a), row(rwkv_r_k.reshape(-1)))
    r, lw, k, v, kk, a, g, bonus = prep
    wkv = _rwkv_recurrence(r, lw, k, v, kk, a)
    o_c = _rwkv_post(wkv, g, bonus, row(rwkv_ln_w), row(rwkv_ln_b))

    merged = _merge(o_a, o_b, o_c, w_branch.astype(BF16), y)
    return _out_proj(merged, w_out.astype(BF16), x, gain_post)


def kernel(x, norm_pre, norm_post, ffn_in, ffn_out, w_in, gla_gate_up, gla_gate_bias, gla_norm, rwkv_mu,
           rwkv_w_up, rwkv_w0, rwkv_a_up, rwkv_a0, rwkv_g_up, rwkv_k_k, rwkv_k_a, rwkv_r_k, rwkv_ln_w,
           rwkv_ln_b, w_branch, w_out):
    batch, seq, d = x.shape
    assert batch == 1 and d == D_MODEL
    h = x[0]
    for l in range(norm_pre.shape[0]):
        h = _ffn(h, norm_pre[l, 0][None, :], norm_post[l, 0][None, :],
                 ffn_in[l, 0].astype(BF16), ffn_out[l, 0].astype(BF16))
        h = _token_mixing(h, norm_pre[l, 1][None, :], w_in[l], gla_gate_up[l], gla_gate_bias[l], gla_norm[l],
                          rwkv_mu[l], rwkv_w_up[l], rwkv_w0[l], rwkv_a_up[l], rwkv_a0[l], rwkv_g_up[l],
                          rwkv_k_k[l], rwkv_k_a[l], rwkv_r_k[l], rwkv_ln_w[l], rwkv_ln_b[l],
                          w_branch[l], w_out[l], norm_post[l, 1][None, :])
        h = _ffn(h, norm_pre[l, 2][None, :], norm_post[l, 2][None, :],
                 ffn_in[l, 1].astype(BF16), ffn_out[l, 1].astype(BF16))
    return h[None]
```

```python
import functools
import math

import jax
import jax.numpy as jnp
from jax import lax
from jax.experimental import pallas as pl
from jax.experimental.pallas import tpu as pltpu

F32 = jnp.float32
BF16 = jnp.bfloat16

D_MODEL = 2048
BRANCH = D_MODEL // 2
D_FF = 256 * math.ceil(8 * D_MODEL / 3 / 256)
FFN_SCALE = 0.5
NORM_EPS = 1e-6

SB_HEAD_DIM = 64
GLA_HEADS = 4
GLA_DV = BRANCH // GLA_HEADS
GLA_DK = GLA_DV // 2
GLA_RANK = 16
GLA_TAU = 16.0
GLA_CHUNK = 64
RW_N = 64
RW_RANK = 64
RW_GATE_RANK = 160
RW_LN_EPS = 64e-5

LANES = 128
MXU_DIM = 256
VMEM_LIMIT_CAP = 60000 * 1024

C_GLA_Q = 0
C_GLA_K = 512
C_GLA_V = 1024
C_GLA_R = 2048
C_RW_R = 3072
C_RW_K = 4096
C_RW_V = 5120
C_GATES = 6144
C_ALPHA = 12288
C_WA = 12416
C_GLOW = 12544
N_REST = 12800
N_SB = 3 * BRANCH


def _vmem_limit(nbytes):
    return int(min(max(2 * nbytes, 32 * 1024 * 1024), VMEM_LIMIT_CAP))


def _params(sem, nbytes):
    return pltpu.CompilerParams(dimension_semantics=sem, vmem_limit_bytes=_vmem_limit(nbytes))


def _rms(x, gain):
    ms = jnp.mean(x * x, axis=-1, keepdims=True)
    return x * lax.rsqrt(ms + NORM_EPS) * gain


def _softplus(x):
    return jnp.maximum(x, 0.0) + jnp.log(1.0 + jnp.exp(-jnp.abs(x)))


def _split(x, n):
    parts = []
    for _ in range(n - 1):
        p = x.astype(BF16)
        parts.append(p)
        x = x - p.astype(F32)
    parts.append(x.astype(BF16))
    return parts


_NN = (((1,), (0,)), ((), ()))
_NT = (((1,), (1,)), ((), ()))


def _mm(a_parts, b_parts, dims=_NN, order=3):
    acc = None
    for i, a in enumerate(a_parts):
        for j, b in enumerate(b_parts):
            if i + j < order:
                t = lax.dot_general(a, b, dims, preferred_element_type=F32)
                acc = t if acc is None else acc + t
    return acc


def _ffn_kernel(x_ref, gpre_ref, gpost_ref, wg_ref, wu_ref, wo_ref, o_ref, h_ref, acc_ref):
    j = pl.program_id(1)

    @pl.when(j == 0)
    def _():
        h_ref[...] = _rms(x_ref[...], gpre_ref[...]).astype(BF16)
        acc_ref[...] = jnp.zeros_like(acc_ref)

    h = h_ref[...]
    g = jnp.dot(h, wg_ref[...], preferred_element_type=F32)
    u = jnp.dot(h, wu_ref[...], preferred_element_type=F32)
    a = (g * jax.nn.sigmoid(g) * u).astype(BF16)
    acc_ref[...] += jnp.dot(a, wo_ref[...], preferred_element_type=F32)

    @pl.when(j == pl.num_programs(1) - 1)
    def _():
        o_ref[...] = x_ref[...] + FFN_SCALE * _rms(acc_ref[...], gpost_ref[...])


def _ffn(x, gpre, gpost, w_in, w_out, tm=512, tf=512):
    s, d = x.shape
    f = w_out.shape[0]
    tm = min(tm, s)
    nj = f // tf
    nbytes = 2 * (2 * tm * d * 4) + tm * d * 6 + 2 * (3 * d * tf * 2) + 4 * tm * tf * 4
    return pl.pallas_call(
        _ffn_kernel,
        grid=(s // tm, nj),
        in_specs=[
            pl.BlockSpec((tm, d), lambda i, j: (i, 0)),
            pl.BlockSpec((1, d), lambda i, j: (0, 0)),
            pl.BlockSpec((1, d), lambda i, j: (0, 0)),
            pl.BlockSpec((d, tf), lambda i, j: (0, j)),
            pl.BlockSpec((d, tf), lambda i, j: (0, j + nj)),
            pl.BlockSpec((tf, d), lambda i, j: (j, 0)),
        ],
        out_specs=pl.BlockSpec((tm, d), lambda i, j: (i, 0)),
        out_shape=jax.ShapeDtypeStruct((s, d), F32),
        scratch_shapes=[pltpu.VMEM((tm, d), BF16), pltpu.VMEM((tm, d), F32)],
        compiler_params=_params(("parallel", "arbitrary"), nbytes),
        name="ffn",
    )(x, gpre, gpost, w_in, w_in, w_out)


def _norm_matmul_kernel(x_ref, g_ref, w_ref, o_ref, h_ref):
    @pl.when(pl.program_id(1) == 0)
    def _():
        h_ref[...] = _rms(x_ref[...], g_ref[...]).astype(BF16)

    o_ref[...] = jnp.dot(h_ref[...], w_ref[...], preferred_element_type=F32).astype(o_ref.dtype)


def _norm_matmul(x, gain, w, out_dtype, tm=1024, tn=512):
    s, d = x.shape
    n = w.shape[1]
    tm = min(tm, s)
    nbytes = 2 * tm * d * 4 + tm * d * 2 + 2 * d * tn * 2 + 2 * tm * tn * 4
    return pl.pallas_call(
        _norm_matmul_kernel,
        grid=(s // tm, n // tn),
        in_specs=[
            pl.BlockSpec((tm, d), lambda i, j: (i, 0)),
            pl.BlockSpec((1, d), lambda i, j: (0, 0)),
            pl.BlockSpec((d, tn), lambda i, j: (0, j)),
        ],
        out_specs=pl.BlockSpec((tm, tn), lambda i, j: (i, j)),
        out_shape=jax.ShapeDtypeStruct((s, n), out_dtype),
        scratch_shapes=[pltpu.VMEM((tm, d), BF16)],
        compiler_params=_params(("parallel", "arbitrary"), nbytes),
        name="norm_matmul",
    )(x, gain, w)


SB_TILE = 256
SB_SUB = 4
SB_EXP_CLAMP = 40.0


def _sb_kernel(q_ref, k_ref, v_ref, o_ref):
    t, nsub = SB_TILE, SB_SUB
    qi = pl.program_id(1)
    left = lax.broadcasted_iota(jnp.int32, (1, LANES), 1) < SB_HEAD_DIM
    q_heads = []
    for s in range(nsub):
        q = q_ref[s * t:(s + 1) * t, :]
        qz = jnp.zeros_like(q)
        q_heads.append((jnp.where(left, q, qz), jnp.where(left, qz, q)))
    row = lax.broadcasted_iota(jnp.int32, (t, t), 0)
    col = lax.broadcasted_iota(jnp.int32, (t, t), 1)
    neg_tri = jnp.where(row >= col, -1.0, 0.0).astype(BF16)
    neg_tri2 = jnp.concatenate([neg_tri, neg_tri], axis=0)
    causal = col < row

    def key_tile(kb, jobs, carry):
        accs, cs = list(carry[0]), [list(c) for c in carry[1]]
        start = pl.multiple_of(kb * t, t)
        k = k_ref[pl.ds(start, t), :]
        v = v_ref[pl.ds(start, t), :]
        vz = jnp.zeros_like(v)
        v_heads = (jnp.where(left, v, vz), jnp.where(left, vz, v))
        chains = [(s, hh, diag) for s, diag in jobs for hh in range(2)]
        z = [lax.dot_general(q_heads[s][hh], k, _NT, preferred_element_type=F32) for s, hh, _ in chains]
        sp = [jnp.maximum(jnp.log(1.0 + jnp.exp(jnp.minimum(x, SB_EXP_CLAMP))), x) for x in z]
        sp = [jnp.where(causal, x, 0.0) if ch[2] else x for x, ch in zip(sp, chains)]
        cum = [jnp.dot(jnp.concatenate(_split(x, 2), axis=1), neg_tri2, preferred_element_type=F32) for x in sp]
        w = [jnp.exp(zz + cc + cs[ch[0]][ch[1]]) for zz, cc, ch in zip(z, cum, chains)]
        w = [jnp.where(causal, x, 0.0) if ch[2] else x for x, ch in zip(w, chains)]
        pv = [jnp.dot(x.astype(BF16), v_heads[ch[1]], preferred_element_type=F32) for x, ch in zip(w, chains)]
        for ch, p, cc in zip(chains, pv, cum):
            accs[ch[0]] = accs[ch[0]] + p
            cs[ch[0]][ch[1]] = cs[ch[0]][ch[1]] + cc[:, :1]
        return tuple(accs), tuple(tuple(c) for c in cs)

    carry = (tuple(jnp.zeros((t, LANES), F32) for _ in range(nsub)),
             tuple((jnp.zeros((t, 1), F32), jnp.zeros((t, 1), F32)) for _ in range(nsub)))
    for d in reversed(range(nsub)):
        carry = key_tile(nsub * qi + d, [(s, s == d) for s in range(d, nsub)], carry)
    everyone = [(s, False) for s in range(nsub)]
    carry = lax.fori_loop(0, nsub * qi, lambda j, c: key_tile(nsub * qi - 1 - j, everyone, c), carry)
    for s in range(nsub):
        o_ref[s * t:(s + 1) * t, :] = carry[0][s].astype(o_ref.dtype)


def _sb_attention(qkv):
    s = qkv.shape[0]
    t = SB_TILE
    tq = SB_SUB * t
    npair = BRANCH // LANES
    nbytes = 2 * 2 * s * LANES * 2 + 16 * SB_SUB * t * t * 4
    return pl.pallas_call(
        _sb_kernel,
        grid=(npair, s // tq),
        in_specs=[
            pl.BlockSpec((tq, LANES), lambda p, i: (i, p)),
            pl.BlockSpec((s, LANES), lambda p, i: (0, npair + p)),
            pl.BlockSpec((s, LANES), lambda p, i: (0, 2 * npair + p)),
        ],
        out_specs=pl.BlockSpec((tq, LANES), lambda p, i: (i, p)),
        out_shape=jax.ShapeDtypeStruct((s, BRANCH), BF16),
        compiler_params=_params(("parallel", "arbitrary"), nbytes),
        name="sb_attention",
    )(qkv, qkv, qkv)


GLA_TILE = 256


def _gla_kernel(q_ref, k_ref, v_ref, r_ref, al_ref, gup_ref, gb_ref, gn_ref, o_ref, st_ref):
    t, c = GLA_TILE, GLA_CHUNK

    @pl.when(pl.program_id(1) == 0)
    def _():
        st_ref[...] = jnp.zeros_like(st_ref)

    x = jnp.dot(al_ref[...].astype(BF16), gup_ref[...], preferred_element_type=F32) + gb_ref[...]
    g = -_softplus(-x) * (1.0 / GLA_TAU)
    row = lax.broadcasted_iota(jnp.int32, (t, t), 0)
    col = lax.broadcasted_iota(jnp.int32, (t, t), 1)
    same = (row // c) == (col // c)
    tril = same & (col <= row)
    g_parts = _split(g, 3)
    b = _mm([jnp.where(tril, 1.0, 0.0).astype(BF16)], g_parts)
    b_last = _mm([jnp.where(same, 1.0, 0.0).astype(BF16)], g_parts)
    q = q_ref[...]
    k = k_ref[...]
    v = v_ref[...].astype(BF16)
    q_dec = (q * (GLA_DK ** -0.5) * jnp.exp(b)).astype(BF16)
    k_dec = (k * jnp.exp(-b)).astype(BF16)
    k_tail = (k * jnp.exp(b_last - b)).astype(BF16)
    chunk_decay = jnp.exp(b_last)
    scores = lax.dot_general(q_dec, k_dec, _NT, preferred_element_type=F32)
    scores = jnp.where(tril, scores, 0.0).astype(BF16)
    o = jnp.dot(scores, v, preferred_element_type=F32)
    outs = []
    st = st_ref[...]
    for n in range(t // c):
        rows = slice(n * c, (n + 1) * c)
        o_inter = lax.dot_general(q_dec[rows], st.astype(BF16), _NT, preferred_element_type=F32)
        outs.append(o[rows] + o_inter)
        kv = jnp.dot(v[rows].astype(F32).T.astype(BF16), k_tail[rows], preferred_element_type=F32)
        st = st * chunk_decay[n * c:n * c + 1, :] + kv
    st_ref[...] = st
    o = jnp.concatenate(outs, axis=0)
    o = o * lax.rsqrt(jnp.mean(o * o, axis=-1, keepdims=True) + NORM_EPS)
    r = r_ref[...]
    o_ref[...] = (o * gn_ref[...] * (r * jax.nn.sigmoid(r))).astype(o_ref.dtype)


def _gla(y, gate_up, gate_bias, gla_norm):
    s = y.shape[0]
    t = min(GLA_TILE, s)
    dk, dv = GLA_DK, GLA_DV
    nbytes = 2 * t * (2 * dk + 2 * dv + LANES) * 4 + 12 * t * t * 4
    return pl.pallas_call(
        _gla_kernel,
        grid=(GLA_HEADS, s // t),
        in_specs=[
            pl.BlockSpec((t, dk), lambda h, i: (i, C_GLA_Q // dk + h)),
            pl.BlockSpec((t, dk), lambda h, i: (i, C_GLA_K // dk + h)),
            pl.BlockSpec((t, dv), lambda h, i: (i, C_GLA_V // dv + h)),
            pl.BlockSpec((t, dv), lambda h, i: (i, C_GLA_R // dv + h)),
            pl.BlockSpec((t, LANES), lambda h, i: (i, C_ALPHA // LANES)),
            pl.BlockSpec((LANES, dk), lambda h, i: (0, h)),
            pl.BlockSpec((1, dk), lambda h, i: (0, h)),
            pl.BlockSpec((1, dv), lambda h, i: (0, h)),
        ],
        out_specs=pl.BlockSpec((t, dv), lambda h, i: (i, h)),
        out_shape=jax.ShapeDtypeStruct((s, BRANCH), BF16),
        scratch_shapes=[pltpu.VMEM((dv, dk), F32)],
        compiler_params=_params(("parallel", "arbitrary"), nbytes),
        name="gla",
    )(y, y, y, y, y, gate_up, gate_bias, gla_norm)


def _head_ones():
    r = lax.broadcasted_iota(jnp.int32, (MXU_DIM, MXU_DIM), 0) // RW_N
    c = lax.broadcasted_iota(jnp.int32, (MXU_DIM, MXU_DIM), 1) // RW_N
    return jnp.where(r == c, 1.0, 0.0).astype(BF16)


def _head_sum(x, ones_bd, n_split):
    outs = []
    for i in range(x.shape[1] // MXU_DIM):
        outs.append(_mm(_split(x[:, i * MXU_DIM:(i + 1) * MXU_DIM], n_split), [ones_bd], order=n_split))
    return jnp.concatenate(outs, axis=1)


def _rwkv_prep_kernel(yr_ref, yk_ref, yv_ref, ywa_ref, yg_ref, pr_ref, pk_ref, pv_ref, pwa_ref, pg_ref,
                      mur_ref, muk_ref, muv_ref, muwa_ref, mug_ref, wup_ref, aup_ref, gup_ref,
                      w0_ref, a0_ref, kk_ref, ka_ref, rk_ref,
                      r_out, lw_out, k_out, v_out, kk_out, a_out, g_out, bonus_out):
    first = pl.program_id(0) == 0

    def shifted(y_ref, p_ref, mu_ref):
        y = y_ref[...]
        prev_row = jnp.where(first, 0.0, p_ref[7:8, :])
        row = lax.broadcasted_iota(jnp.int32, y.shape, 0)
        y_prev = jnp.where(row == 0, prev_row, pltpu.roll(y, 1, axis=0))
        return y + (y_prev - y) * mu_ref[...]

    r = shifted(yr_ref, pr_ref, mur_ref)
    k = shifted(yk_ref, pk_ref, muk_ref)
    v = shifted(yv_ref, pv_ref, muv_ref)
    wa = shifted(ywa_ref, pwa_ref, muwa_ref)
    gl = shifted(yg_ref, pg_ref, mug_ref)
    w_pre = w0_ref[...] + jnp.dot(jnp.tanh(wa).astype(BF16), wup_ref[...], preferred_element_type=F32)
    lw = -jnp.exp(-_softplus(-w_pre) - 0.5)
    a = jax.nn.sigmoid(a0_ref[...] + jnp.dot(wa.astype(BF16), aup_ref[...], preferred_element_type=F32))
    g = jnp.dot(jax.nn.sigmoid(gl).astype(BF16), gup_ref[...], preferred_element_type=F32)
    ones_bd = _head_ones()
    kk = k * kk_ref[...]
    kk = kk / jnp.maximum(jnp.sqrt(_head_sum(kk * kk, ones_bd, 3)), 1e-12)
    k = k * (1.0 + (a - 1.0) * ka_ref[...])
    bonus = _head_sum(r * k * rk_ref[...], ones_bd, 3) * v
    r_out[...] = r
    lw_out[...] = lw
    k_out[...] = k
    v_out[...] = v
    kk_out[...] = kk
    a_out[...] = a
    g_out[...] = g
    bonus_out[...] = bonus


def _rwkv_prep(y, mu_r, mu_k, mu_v, mu_wa, mu_g, wup, aup, gup, w0, a0, k_k, k_a, r_k, tm=512):
    s = y.shape[0]
    tm = min(tm, s)
    w = BRANCH

    def cur(width, col):
        return pl.BlockSpec((tm, width), lambda i: (i, col // width))

    def prev(width, col):
        return pl.BlockSpec((8, width), lambda i: (jnp.maximum(i * (tm // 8) - 1, 0), col // width))

    def full(shape):
        return pl.BlockSpec(shape, lambda i: (0, 0))

    nbytes = 2 * tm * (3 * w + 384) * 4 + 2 * 8 * tm * w * 4 + 12 * tm * w * 4
    out = jax.ShapeDtypeStruct((s, w), F32)
    return pl.pallas_call(
        _rwkv_prep_kernel,
        grid=(s // tm,),
        in_specs=[cur(w, C_RW_R), cur(w, C_RW_K), cur(w, C_RW_V), cur(LANES, C_WA), cur(2 * LANES, C_GLOW),
                  prev(w, C_RW_R), prev(w, C_RW_K), prev(w, C_RW_V), prev(LANES, C_WA), prev(2 * LANES, C_GLOW),
                  full((1, w)), full((1, w)), full((1, w)), full((1, LANES)), full((1, 2 * LANES)),
                  full((LANES, w)), full((LANES, w)), full((2 * LANES, w)),
                  full((1, w)), full((1, w)), full((1, w)), full((1, w)), full((1, w))],
        out_specs=[pl.BlockSpec((tm, w), lambda i: (i, 0))] * 8,
        out_shape=[out] * 8,
        compiler_params=_params(("parallel",), nbytes),
        name="rwkv_prep",
    )(y, y, y, y, y, y, y, y, y, y, mu_r, mu_k, mu_v, mu_wa, mu_g, wup, aup, gup, w0, a0, k_k, k_a, r_k)


RW_CHUNK = 64
RW_PACK = MXU_DIM // RW_N


def _rwkv_kernel(r_ref, lw_ref, k_ref, v_ref, kk_ref, a_ref, o_ref, s_ref):
    c, g4 = RW_CHUNK, MXU_DIM

    @pl.when(pl.program_id(0) == 0)
    def _():
        s_ref[...] = jnp.zeros_like(s_ref)

    ti = lax.broadcasted_iota(jnp.int32, (c, g4), 0)
    ci = lax.broadcasted_iota(jnp.int32, (c, g4), 1) % c
    strict = ci < ti
    incl = ci <= ti
    lane_head = lax.broadcasted_iota(jnp.int32, (1, g4), 1) // RW_N
    head_masks = [lane_head == h for h in range(RW_PACK)]
    bd_mask = (lax.broadcasted_iota(jnp.int32, (g4, g4), 0) // RW_N) == (lax.broadcasted_iota(jnp.int32, (g4, g4), 1) // RW_N)
    tr = lax.broadcasted_iota(jnp.int32, (c, c), 0)
    tc = lax.broadcasted_iota(jnp.int32, (c, c), 1)
    tri = jnp.where(tc <= tr, 1.0, 0.0).astype(BF16)

    def stack(parts):
        return [jnp.concatenate([jnp.where(m, p, jnp.zeros_like(p)) for m in head_masks], axis=0) for p in parts]

    groups = range(BRANCH // g4)
    lanes = [slice(grp * g4, (grp + 1) * g4) for grp in groups]

    def each(fn, *cols):
        return [fn(*args) for args in zip(*cols)]

    def mm2(a, b, dims=_NN):
        return _mm(_split(a, 2), _split(b, 2), dims)

    def mm2s(a, b, dims=_NN):
        return _mm(_split(a, 2), stack(_split(b, 2)), dims)

    lw_all = lw_ref[...]
    cum_all = _mm([tri], _split(lw_all, 3))
    lw = [lw_all[:, sl] for sl in lanes]
    cum = [cum_all[:, sl] for sl in lanes]
    cum_end = [x[c - 1:c, :] for x in cum]
    kk = [kk_ref[:, sl] for sl in lanes]
    kka = [x * a_ref[:, sl] for x, sl in zip(kk, lanes)]
    k = [k_ref[:, sl] for sl in lanes]
    v = [v_ref[:, sl] for sl in lanes]
    g_inv = each(lambda x: jnp.exp(-x), cum)
    tail = each(lambda e, x: jnp.exp(e - x), cum_end, cum)
    alp = each(lambda x, cm, l: x * jnp.exp(cm - l), kk, cum, lw)
    rho = each(lambda sl, cm: r_ref[:, sl] * jnp.exp(cm), lanes, cum)
    ar = each(lambda x, y: _split(jnp.concatenate([x, y], axis=0), 2), alp, rho)
    pb = each(lambda x, y, gi: _mm(x, stack(_split(y * gi, 2)), _NT), ar, kka, g_inv)
    pk = each(lambda x, y, gi: _mm(x, stack(_split(y * gi, 2)), _NT), ar, k, g_inv)
    l_b = each(lambda p: jnp.where(strict, p[:c], 0.0), pb)
    m_b = each(lambda p: jnp.where(incl, p[c:], 0.0), pb)
    l_k = each(lambda p: jnp.where(strict, p[:c], 0.0), pk)
    m_k = each(lambda p: jnp.where(incl, p[c:], 0.0), pk)

    x = [jnp.where(ci == ti, 1.0, 0.0) for _ in groups]
    m = 1
    while m < c:
        lower = ((ti // m) % 2 == 1) & ((ci // m) == (ti // m) - 1)
        y = each(lambda xx, ll: mm2s(xx, jnp.where(lower, ll, 0.0)), x, l_b)
        x = each(lambda xx, yy: xx - mm2s(yy, xx), x, y)
        m *= 2

    st = [s_ref[grp] for grp in groups]
    ars = each(lambda a, s: _mm(a, _split(s, 2), _NT), ar, st)
    v_stack = each(lambda t: stack(_split(t, 2)), v)
    x0 = each(lambda a, l, vs: a[:c] + _mm(_split(l, 2), vs), ars, l_k, v_stack)
    u = each(mm2s, x, x0)
    o = each(lambda a, mb, uu, mk, vs: a[c:] - mm2s(mb, uu) + _mm(_split(mk, 2), vs), ars, m_b, u, m_k, v_stack)
    vu_t = each(lambda vv, uu: jnp.concatenate([vv, -uu], axis=0).T, v, u)
    kb = each(lambda kx, ka, t: jnp.concatenate([kx * t, ka * t], axis=0), k, kka, tail)
    st_new = each(lambda s, e, a, b: s * jnp.exp(e) + mm2(a, b), st, cum_end, vu_t, kb)
    for grp in groups:
        s_ref[grp] = jnp.where(bd_mask, st_new[grp], 0.0)
        o_ref[:, lanes[grp]] = o[grp]


def _rwkv_recurrence(r, lw, k, v, kk, a):
    s, w = r.shape
    c = RW_CHUNK
    spec = pl.BlockSpec((c, w), lambda i: (i, 0))
    nbytes = 2 * 7 * c * w * 4 + (w // MXU_DIM) * MXU_DIM * MXU_DIM * 4 + 64 * MXU_DIM * MXU_DIM * 4
    return pl.pallas_call(
        _rwkv_kernel,
        grid=(s // c,),
        in_specs=[spec] * 6,
        out_specs=spec,
        out_shape=jax.ShapeDtypeStruct((s, w), F32),
        scratch_shapes=[pltpu.VMEM((w // MXU_DIM, MXU_DIM, MXU_DIM), F32)],
        compiler_params=_params(("arbitrary",), nbytes),
        name="rwkv_recurrence",
    )(r, lw, k, v, kk, a)


def _rwkv_post_kernel(wkv_ref, g_ref, bonus_ref, lnw_ref, lnb_ref, o_ref):
    ones_bd = _head_ones()
    x = wkv_ref[...]
    d = x - _head_sum(x, ones_bd, 3) * (1.0 / RW_N)
    var = _head_sum(d * d, ones_bd, 3) * (1.0 / RW_N)
    y = d * lax.rsqrt(var + RW_LN_EPS) * lnw_ref[...] + lnb_ref[...]
    o_ref[...] = ((y + bonus_ref[...]) * g_ref[...]).astype(o_ref.dtype)


def _rwkv_post(wkv, g, bonus, ln_w, ln_b, tm=512):
    s, w = wkv.shape
    tm = min(tm, s)
    blk = pl.BlockSpec((tm, w), lambda i: (i, 0))
    vec = pl.BlockSpec((1, w), lambda i: (0, 0))
    nbytes = 2 * 4 * tm * w * 4 + 8 * tm * w * 4
    return pl.pallas_call(
        _rwkv_post_kernel,
        grid=(s // tm,),
        in_specs=[blk, blk, blk, vec, vec],
        out_specs=blk,
        out_shape=jax.ShapeDtypeStruct((s, w), BF16),
        compiler_params=_params(("parallel",), nbytes),
        name="rwkv_post",
    )(wkv, g, bonus, ln_w, ln_b)


def _merge_kernel(oa_ref, ob_ref, oc_ref, wb_ref, ga_ref, gb_ref, gc_ref, o_ref):
    acc = None
    for o_g, gate, idx in ((oa_ref, ga_ref, 0), (ob_ref, gb_ref, 1), (oc_ref, gc_ref, 2)):
        u = jnp.dot(o_g[...], wb_ref[idx], preferred_element_type=F32)
        t = jax.nn.sigmoid(gate[...]) * u
        acc = t if acc is None else acc + t
    o_ref[...] = acc.astype(o_ref.dtype)


def _merge(oa, ob, oc, w_branch, y, tm=512, tn=512):
    s, w = oa.shape
    d = w_branch.shape[2]
    tm = min(tm, s)
    branch = pl.BlockSpec((tm, w), lambda i, j: (i, 0))

    def gate(gi):
        return pl.BlockSpec((tm, tn), lambda i, j: (i, (C_GATES + gi * d) // tn + j))

    nbytes = 2 * 3 * tm * w * 2 + 2 * 3 * w * tn * 2 + 2 * 4 * tm * tn * 4 + 4 * tm * tn * 4
    return pl.pallas_call(
        _merge_kernel,
        grid=(s // tm, d // tn),
        in_specs=[branch, branch, branch, pl.BlockSpec((3, w, tn), lambda i, j: (0, 0, j)),
                  gate(0), gate(1), gate(2)],
        out_specs=pl.BlockSpec((tm, tn), lambda i, j: (i, j)),
        out_shape=jax.ShapeDtypeStruct((s, d), BF16),
        compiler_params=_params(("parallel", "arbitrary"), nbytes),
        name="merge",
    )(oa, ob, oc, w_branch, y, y, y)


def _out_proj_kernel(m_ref, w_ref, x_ref, g_ref, o_ref):
    mix = jnp.dot(m_ref[...], w_ref[...], preferred_element_type=F32)
    o_ref[...] = x_ref[...] + _rms(mix, g_ref[...])


def _out_proj(merged, w_out, x, gain, tm=512):
    s, d = x.shape
    tm = min(tm, s)
    nbytes = 2 * tm * d * 2 + 2 * d * d * 2 + 4 * tm * d * 4 + 2 * tm * d * 4
    return pl.pallas_call(
        _out_proj_kernel,
        grid=(s // tm,),
        in_specs=[pl.BlockSpec((tm, d), lambda i: (i, 0)), pl.BlockSpec((d, d), lambda i: (0, 0)),
                  pl.BlockSpec((tm, d), lambda i: (i, 0)), pl.BlockSpec((1, d), lambda i: (0, 0))],
        out_specs=pl.BlockSpec((tm, d), lambda i: (i, 0)),
        out_shape=jax.ShapeDtypeStruct((s, d), F32),
        compiler_params=_params(("parallel",), nbytes),
        name="out_proj",
    )(merged, w_out, x, gain)


def _pad_cols(w, width):
    return jnp.pad(w, ((0, 0), (0, width - w.shape[1])))


def _pad_rows(w, height, offset=0):
    return jnp.pad(w, ((offset, height - offset - w.shape[0]), (0, 0)))


def _pack_w_in(w):
    b = BRANCH
    gla0 = 3 * b
    alpha0 = gla0 + 2 * GLA_HEADS * GLA_DK + 2 * b
    rw0 = alpha0 + GLA_RANK
    wa0 = rw0 + 3 * b
    gl0 = wa0 + 2 * RW_RANK
    gates0 = gl0 + RW_GATE_RANK
    w_sb = jnp.concatenate([w[:, :b] * (SB_HEAD_DIM ** -0.5), w[:, b:3 * b]], axis=1).astype(BF16)
    w_rest = jnp.concatenate([
        w[:, gla0:alpha0], w[:, rw0:wa0], w[:, gates0:],
        _pad_cols(w[:, alpha0:rw0], LANES), w[:, wa0:gl0], _pad_cols(w[:, gl0:gates0], 2 * LANES)],
        axis=1).astype(BF16)
    return w_sb, w_rest


def _token_mixing(x, gain, w_in, gla_gate_up, gla_gate_bias, gla_norm, rwkv_mu, rwkv_w_up, rwkv_w0,
                  rwkv_a_up, rwkv_a0, rwkv_g_up, rwkv_k_k, rwkv_k_a, rwkv_r_k, rwkv_ln_w, rwkv_ln_b,
                  w_branch, w_out, gain_post):
    b = BRANCH
    w_sb, w_rest = _pack_w_in(w_in)
    qkv = _norm_matmul(x, gain, w_sb, BF16)
    y = _norm_matmul(x, gain, w_rest, F32)

    o_a = _sb_attention(qkv)

    o_b = _gla(y, _pad_rows(gla_gate_up, LANES).astype(BF16), gla_gate_bias[None, :], gla_norm[None, :])

    mu = rwkv_mu
    row = lambda t: t[None, :]
    prep = _rwkv_prep(
        y, row(mu[:b]), row(mu[b:2 * b]), row(mu[2 * b:3 * b]), row(mu[3 * b:3 * b + 2 * RW_RANK]),
        _pad_cols(row(mu[3 * b + 2 * RW_RANK:]), 2 * LANES),
        _pad_rows(rwkv_w_up, LANES).astype(BF16), _pad_rows(rwkv_a_up, LANES, RW_RANK).astype(BF16),
        _pad_rows(rwkv_g_up, 2 * LANES).astype(BF16),
        row(rwkv_w0), row(rwkv_a0), row(rwkv_k_k), row(rwkv_k_a), row(rwkv_r_k.reshape(-1)))
    r, lw, k, v, kk, a, g, bonus = prep
    wkv = _rwkv_recurrence(r, lw, k, v, kk, a)
    o_c = _rwkv_post(wkv, g, bonus, row(rwkv_ln_w), row(rwkv_ln_b))

    merged = _merge(o_a, o_b, o_c, w_branch.astype(BF16), y)
    return _out_proj(merged, w_out.astype(BF16), x, gain_post)


def kernel(x, norm_pre, norm_post, ffn_in, ffn_out, w_in, gla_gate_up, gla_gate_bias, gla_norm, rwkv_mu,
           rwkv_w_up, rwkv_w0, rwkv_a_up, rwkv_a0, rwkv_g_up, rwkv_k_k, rwkv_k_a, rwkv_r_k, rwkv_ln_w,
           rwkv_ln_b, w_branch, w_out):
    batch, seq, d = x.shape
    assert batch == 1 and d == D_MODEL
    h = x[0]
    for l in range(norm_pre.shape[0]):
        h = _ffn(h, norm_pre[l, 0][None, :], norm_post[l, 0][None, :],
                 ffn_in[l, 0].astype(BF16), ffn_out[l, 0].astype(BF16))
        h = _token_mixing(h, norm_pre[l, 1][None, :], w_in[l], gla_gate_up[l], gla_gate_bias[l], gla_norm[l],
                          rwkv_mu[l], rwkv_w_up[l], rwkv_w0[l], rwkv_a_up[l], rwkv_a0[l], rwkv_g_up[l],
                          rwkv_k_k[l], rwkv_k_a[l], rwkv_r_k[l], rwkv_ln_w[l], rwkv_ln_b[l],
                          w_branch[l], w_out[l], norm_post[l, 1][None, :])
        h = _ffn(h, norm_pre[l, 2][None, :], norm_post[l, 2][None, :],
                 ffn_in[l, 1].astype(BF16), ffn_out[l, 1].astype(BF16))
    return h[None]
```

```python
import functools
import math

import jax
import jax.numpy as jnp
from jax import lax
from jax.experimental import pallas as pl
from jax.experimental.pallas import tpu as pltpu

F32 = jnp.float32
BF16 = jnp.bfloat16

D_MODEL = 2048
BRANCH = D_MODEL // 2
D_FF = 256 * math.ceil(8 * D_MODEL / 3 / 256)
FFN_SCALE = 0.5
NORM_EPS = 1e-6

SB_HEAD_DIM = 64
GLA_HEADS = 4
GLA_DV = BRANCH // GLA_HEADS
GLA_DK = GLA_DV // 2
GLA_RANK = 16
GLA_TAU = 16.0
GLA_CHUNK = 64
RW_N = 64
RW_RANK = 64
RW_GATE_RANK = 160
RW_LN_EPS = 64e-5

LANES = 128
MXU_DIM = 256
VMEM_LIMIT_CAP = 60000 * 1024

C_GLA_Q = 0
C_GLA_K = 512
C_GLA_V = 1024
C_GLA_R = 2048
C_RW_R = 3072
C_RW_K = 4096
C_RW_V = 5120
C_GATES = 6144
C_ALPHA = 12288
C_WA = 12416
C_GLOW = 12544
N_REST = 12800
N_SB = 3 * BRANCH


def _vmem_limit(nbytes):
    return int(min(max(2 * nbytes, 32 * 1024 * 1024), VMEM_LIMIT_CAP))


def _params(sem, nbytes):
    return pltpu.CompilerParams(dimension_semantics=sem, vmem_limit_bytes=_vmem_limit(nbytes))


def _rms(x, gain):
    ms = jnp.mean(x * x, axis=-1, keepdims=True)
    return x * lax.rsqrt(ms + NORM_EPS) * gain


def _softplus(x):
    return jnp.maximum(x, 0.0) + jnp.log(1.0 + jnp.exp(-jnp.abs(x)))


def _split(x, n):
    parts = []
    for _ in range(n - 1):
        p = x.astype(BF16)
        parts.append(p)
        x = x - p.astype(F32)
    parts.append(x.astype(BF16))
    return parts


_NN = (((1,), (0,)), ((), ()))
_NT = (((1,), (1,)), ((), ()))


def _mm(a_parts, b_parts, dims=_NN, order=3):
    acc = None
    for i, a in enumerate(a_parts):
        for j, b in enumerate(b_parts):
            if i + j < order:
                t = lax.dot_general(a, b, dims, preferred_element_type=F32)
                acc = t if acc is None else acc + t
    return acc


def _ffn_kernel(x_ref, gpre_ref, gpost_ref, wg_ref, wu_ref, wo_ref, o_ref, h_ref, acc_ref):
    j = pl.program_id(1)

    @pl.when(j == 0)
    def _():
        h_ref[...] = _rms(x_ref[...], gpre_ref[...]).astype(BF16)
        acc_ref[...] = jnp.zeros_like(acc_ref)

    h = h_ref[...]
    g = jnp.dot(h, wg_ref[...], preferred_element_type=F32)
    u = jnp.dot(h, wu_ref[...], preferred_element_type=F32)
    a = (g * jax.nn.sigmoid(g) * u).astype(BF16)
    acc_ref[...] += jnp.dot(a, wo_ref[...], preferred_element_type=F32)

    @pl.when(j == pl.num_programs(1) - 1)
    def _():
        o_ref[...] = x_ref[...] + FFN_SCALE * _rms(acc_ref[...], gpost_ref[...])


def _ffn(x, gpre, gpost, w_in, w_out, tm=512, tf=512):
    s, d = x.shape
    f = w_out.shape[0]
    tm = min(tm, s)
    nj = f // tf
    nbytes = 2 * (2 * tm * d * 4) + tm * d * 6 + 2 * (3 * d * tf * 2) + 4 * tm * tf * 4
    return pl.pallas_call(
        _ffn_kernel,
        grid=(s // tm, nj),
        in_specs=[
            pl.BlockSpec((tm, d), lambda i, j: (i, 0)),
            pl.BlockSpec((1, d), lambda i, j: (0, 0)),
            pl.BlockSpec((1, d), lambda i, j: (0, 0)),
            pl.BlockSpec((d, tf), lambda i, j: (0, j)),
            pl.BlockSpec((d, tf), lambda i, j: (0, j + nj)),
            pl.BlockSpec((tf, d), lambda i, j: (j, 0)),
        ],
        out_specs=pl.BlockSpec((tm, d), lambda i, j: (i, 0)),
        out_shape=jax.ShapeDtypeStruct((s, d), F32),
        scratch_shapes=[pltpu.VMEM((tm, d), BF16), pltpu.VMEM((tm, d), F32)],
        compiler_params=_params(("parallel", "arbitrary"), nbytes),
        name="ffn",
    )(x, gpre, gpost, w_in, w_in, w_out)


def _norm_matmul_kernel(x_ref, g_ref, w_ref, o_ref, h_ref):
    @pl.when(pl.program_id(1) == 0)
    def _():
        h_ref[...] = _rms(x_ref[...], g_ref[...]).astype(BF16)

    o_ref[...] = jnp.dot(h_ref[...], w_ref[...], preferred_element_type=F32).astype(o_ref.dtype)


def _norm_matmul(x, gain, w, out_dtype, tm=1024, tn=512):
    s, d = x.shape
    n = w.shape[1]
    tm = min(tm, s)
    nbytes = 2 * tm * d * 4 + tm * d * 2 + 2 * d * tn * 2 + 2 * tm * tn * 4
    return pl.pallas_call(
        _norm_matmul_kernel,
        grid=(s // tm, n // tn),
        in_specs=[
            pl.BlockSpec((tm, d), lambda i, j: (i, 0)),
            pl.BlockSpec((1, d), lambda i, j: (0, 0)),
            pl.BlockSpec((d, tn), lambda i, j: (0, j)),
        ],
        out_specs=pl.BlockSpec((tm, tn), lambda i, j: (i, j)),
        out_shape=jax.ShapeDtypeStruct((s, n), out_dtype),
        scratch_shapes=[pltpu.VMEM((tm, d), BF16)],
        compiler_params=_params(("parallel", "arbitrary"), nbytes),
        name="norm_matmul",
    )(x, gain, w)


SB_TILE = 256
SB_SUB = 4
SB_EXP_CLAMP = 40.0
SB_LOG_ZERO = -110.0
SB_NO_KEYS = -1e30


def _sb_kernel(q_ref, k_ref, v_ref, o_ref):
    t, nsub = SB_TILE, SB_SUB
    first = nsub * pl.program_id(1)
    left = lax.broadcasted_iota(jnp.int32, (1, LANES), 1) < SB_HEAD_DIM
    q_heads = []
    for s in range(nsub):
        q = q_ref[s * t:(s + 1) * t, :]
        qz = jnp.zeros_like(q)
        q_heads.append((jnp.where(left, q, qz), jnp.where(left, qz, q)))
    row = lax.broadcasted_iota(jnp.int32, (t, t), 0)
    col = lax.broadcasted_iota(jnp.int32, (t, t), 1)
    neg_tri = jnp.where(row >= col, -1.0, 0.0).astype(BF16)
    neg_tri2 = jnp.concatenate([neg_tri, neg_tri], axis=0)
    causal = col < row
    chains = [(s, hh) for s in range(nsub) for hh in range(2)]

    def step(back, diagonal, accs, cs):
        ks, vs = [], []
        for s in range(nsub):
            kb = first + s - back
            if not diagonal:
                cs = [jnp.where(kb >= 0, c, SB_NO_KEYS) if ch[0] == s else c for c, ch in zip(cs, chains)]
                kb = jnp.maximum(kb, 0)
            start = pl.multiple_of(kb * t, t)
            ks.append(k_ref[pl.ds(start, t), :])
            v = v_ref[pl.ds(start, t), :]
            vz = jnp.zeros_like(v)
            vs.append((jnp.where(left, v, vz), jnp.where(left, vz, v)))
        z = [lax.dot_general(q_heads[s][hh], ks[s], _NT, preferred_element_type=F32) for s, hh in chains]
        sp = [jnp.maximum(jnp.log(1.0 + jnp.exp(jnp.minimum(x, SB_EXP_CLAMP))), x) for x in z]
        if diagonal:
            sp = [jnp.where(causal, x, 0.0) for x in sp]
        cum = [jnp.dot(jnp.concatenate(_split(x, 2), axis=1), neg_tri2, preferred_element_type=F32) for x in sp]
        w = [jnp.exp(zz + cc + c) for zz, cc, c in zip(z, cum, cs)]
        if diagonal:
            w = [jnp.where(causal, x, 0.0) for x in w]
        pv = [jnp.dot(x.astype(BF16), vs[s][hh], preferred_element_type=F32) for x, (s, hh) in zip(w, chains)]
        accs = [accs[s] + pv[2 * s] + pv[2 * s + 1] for s in range(nsub)]
        cs = [c + cc[:, :1] for c, cc in zip(cs, cum)]
        live = jnp.max(functools.reduce(jnp.maximum, cs)) >= SB_LOG_ZERO
        return accs, cs, live.astype(jnp.int32)

    accs = [jnp.zeros((t, LANES), F32) for _ in range(nsub)]
    cs = [jnp.zeros((t, 1), F32) for _ in chains]
    accs, cs, live = step(0, True, accs, cs)

    def more(carry):
        return jnp.logical_and(carry[0] < first + nsub, carry[1] > 0)

    def walk(carry):
        accs, cs, live = step(carry[0], False, list(carry[2]), list(carry[3]))
        return carry[0] + 1, live, tuple(accs), tuple(cs)

    carry = lax.while_loop(more, walk, (jnp.int32(1), live, tuple(accs), tuple(cs)))
    for s in range(nsub):
        o_ref[s * t:(s + 1) * t, :] = carry[2][s].astype(o_ref.dtype)


def _sb_attention(qkv):
    s = qkv.shape[0]
    t = SB_TILE
    tq = SB_SUB * t
    npair = BRANCH // LANES
    nbytes = 2 * 2 * s * LANES * 2 + 16 * SB_SUB * t * t * 4
    return pl.pallas_call(
        _sb_kernel,
        grid=(npair, s // tq),
        in_specs=[
            pl.BlockSpec((tq, LANES), lambda p, i: (i, p)),
            pl.BlockSpec((s, LANES), lambda p, i: (0, npair + p)),
            pl.BlockSpec((s, LANES), lambda p, i: (0, 2 * npair + p)),
        ],
        out_specs=pl.BlockSpec((tq, LANES), lambda p, i: (i, p)),
        out_shape=jax.ShapeDtypeStruct((s, BRANCH), BF16),
        compiler_params=_params(("parallel", "arbitrary"), nbytes),
        name="sb_attention",
    )(qkv, qkv, qkv)


GLA_TILE = 256


def _gla_kernel(q_ref, k_ref, v_ref, r_ref, al_ref, gup_ref, gb_ref, gn_ref, o_ref, st_ref):
    t, c = GLA_TILE, GLA_CHUNK

    @pl.when(pl.program_id(1) == 0)
    def _():
        st_ref[...] = jnp.zeros_like(st_ref)

    x = jnp.dot(al_ref[...].astype(BF16), gup_ref[...], preferred_element_type=F32) + gb_ref[...]
    g = -_softplus(-x) * (1.0 / GLA_TAU)
    row = lax.broadcasted_iota(jnp.int32, (t, t), 0)
    col = lax.broadcasted_iota(jnp.int32, (t, t), 1)
    same = (row // c) == (col // c)
    tril = same & (col <= row)
    g_parts = _split(g, 3)
    b = _mm([jnp.where(tril, 1.0, 0.0).astype(BF16)], g_parts)
    b_last = _mm([jnp.where(same, 1.0, 0.0).astype(BF16)], g_parts)
    q = q_ref[...]
    k = k_ref[...]
    v = v_ref[...].astype(BF16)
    q_dec = (q * (GLA_DK ** -0.5) * jnp.exp(b)).astype(BF16)
    k_dec = (k * jnp.exp(-b)).astype(BF16)
    k_tail = (k * jnp.exp(b_last - b)).astype(BF16)
    chunk_decay = jnp.exp(b_last)
    scores = lax.dot_general(q_dec, k_dec, _NT, preferred_element_type=F32)
    scores = jnp.where(tril, scores, 0.0).astype(BF16)
    o = jnp.dot(scores, v, preferred_element_type=F32)
    outs = []
    st = st_ref[...]
    for n in range(t // c):
        rows = slice(n * c, (n + 1) * c)
        o_inter = lax.dot_general(q_dec[rows], st.astype(BF16), _NT, preferred_element_type=F32)
        outs.append(o[rows] + o_inter)
        kv = jnp.dot(v[rows].astype(F32).T.astype(BF16), k_tail[rows], preferred_element_type=F32)
        st = st * chunk_decay[n * c:n * c + 1, :] + kv
    st_ref[...] = st
    o = jnp.concatenate(outs, axis=0)
    o = o * lax.rsqrt(jnp.mean(o * o, axis=-1, keepdims=True) + NORM_EPS)
    r = r_ref[...]
    o_ref[...] = (o * gn_ref[...] * (r * jax.nn.sigmoid(r))).astype(o_ref.dtype)


def _gla(y, gate_up, gate_bias, gla_norm):
    s = y.shape[0]
    t = min(GLA_TILE, s)
    dk, dv = GLA_DK, GLA_DV
    nbytes = 2 * t * (2 * dk + 2 * dv + LANES) * 4 + 12 * t * t * 4
    return pl.pallas_call(
        _gla_kernel,
        grid=(GLA_HEADS, s // t),
        in_specs=[
            pl.BlockSpec((t, dk), lambda h, i: (i, C_GLA_Q // dk + h)),
            pl.BlockSpec((t, dk), lambda h, i: (i, C_GLA_K // dk + h)),
            pl.BlockSpec((t, dv), lambda h, i: (i, C_GLA_V // dv + h)),
            pl.BlockSpec((t, dv), lambda h, i: (i, C_GLA_R // dv + h)),
            pl.BlockSpec((t, LANES), lambda h, i: (i, C_ALPHA // LANES)),
            pl.BlockSpec((LANES, dk), lambda h, i: (0, h)),
            pl.BlockSpec((1, dk), lambda h, i: (0, h)),
            pl.BlockSpec((1, dv), lambda h, i: (0, h)),
        ],
        out_specs=pl.BlockSpec((t, dv), lambda h, i: (i, h)),
        out_shape=jax.ShapeDtypeStruct((s, BRANCH), BF16),
        scratch_shapes=[pltpu.VMEM((dv, dk), F32)],
        compiler_params=_params(("parallel", "arbitrary"), nbytes),
        name="gla",
    )(y, y, y, y, y, gate_up, gate_bias, gla_norm)


def _head_ones():
    r = lax.broadcasted_iota(jnp.int32, (MXU_DIM, MXU_DIM), 0) // RW_N
    c = lax.broadcasted_iota(jnp.int32, (MXU_DIM, MXU_DIM), 1) // RW_N
    return jnp.where(r == c, 1.0, 0.0).astype(BF16)


def _head_sum(x, ones_bd, n_split):
    outs = []
    for i in range(x.shape[1] // MXU_DIM):
        outs.append(_mm(_split(x[:, i * MXU_DIM:(i + 1) * MXU_DIM], n_split), [ones_bd], order=n_split))
    return jnp.concatenate(outs, axis=1)


def _rwkv_prep_kernel(yr_ref, yk_ref, yv_ref, ywa_ref, yg_ref, pr_ref, pk_ref, pv_ref, pwa_ref, pg_ref,
                      mur_ref, muk_ref, muv_ref, muwa_ref, mug_ref, wup_ref, aup_ref, gup_ref,
                      w0_ref, a0_ref, kk_ref, ka_ref, rk_ref,
                      r_out, lw_out, k_out, v_out, kk_out, a_out, g_out, bonus_out):
    first = pl.program_id(0) == 0

    def shifted(y_ref, p_ref, mu_ref):
        y = y_ref[...]
        prev_row = jnp.where(first, 0.0, p_ref[7:8, :])
        row = lax.broadcasted_iota(jnp.int32, y.shape, 0)
        y_prev = jnp.where(row == 0, prev_row, pltpu.roll(y, 1, axis=0))
        return y + (y_prev - y) * mu_ref[...]

    r = shifted(yr_ref, pr_ref, mur_ref)
    k = shifted(yk_ref, pk_ref, muk_ref)
    v = shifted(yv_ref, pv_ref, muv_ref)
    wa = shifted(ywa_ref, pwa_ref, muwa_ref)
    gl = shifted(yg_ref, pg_ref, mug_ref)
    w_pre = w0_ref[...] + jnp.dot(jnp.tanh(wa).astype(BF16), wup_ref[...], preferred_element_type=F32)
    lw = -jnp.exp(-_softplus(-w_pre) - 0.5)
    a = jax.nn.sigmoid(a0_ref[...] + jnp.dot(wa.astype(BF16), aup_ref[...], preferred_element_type=F32))
    g = jnp.dot(jax.nn.sigmoid(gl).astype(BF16), gup_ref[...], preferred_element_type=F32)
    ones_bd = _head_ones()
    kk = k * kk_ref[...]
    kk = kk / jnp.maximum(jnp.sqrt(_head_sum(kk * kk, ones_bd, 3)), 1e-12)
    k = k * (1.0 + (a - 1.0) * ka_ref[...])
    bonus = _head_sum(r * k * rk_ref[...], ones_bd, 3) * v
    r_out[...] = r
    lw_out[...] = lw
    k_out[...] = k
    v_out[...] = v
    kk_out[...] = kk
    a_out[...] = a
    g_out[...] = g
    bonus_out[...] = bonus


def _rwkv_prep(y, mu_r, mu_k, mu_v, mu_wa, mu_g, wup, aup, gup, w0, a0, k_k, k_a, r_k, tm=512):
    s = y.shape[0]
    tm = min(tm, s)
    w = BRANCH

    def cur(width, col):
        return pl.BlockSpec((tm, width), lambda i: (i, col // width))

    def prev(width, col):
        return pl.BlockSpec((8, width), lambda i: (jnp.maximum(i * (tm // 8) - 1, 0), col // width))

    def full(shape):
        return pl.BlockSpec(shape, lambda i: (0, 0))

    nbytes = 2 * tm * (3 * w + 384) * 4 + 2 * 8 * tm * w * 4 + 12 * tm * w * 4
    out = jax.ShapeDtypeStruct((s, w), F32)
    return pl.pallas_call(
        _rwkv_prep_kernel,
        grid=(s // tm,),
        in_specs=[cur(w, C_RW_R), cur(w, C_RW_K), cur(w, C_RW_V), cur(LANES, C_WA), cur(2 * LANES, C_GLOW),
                  prev(w, C_RW_R), prev(w, C_RW_K), prev(w, C_RW_V), prev(LANES, C_WA), prev(2 * LANES, C_GLOW),
                  full((1, w)), full((1, w)), full((1, w)), full((1, LANES)), full((1, 2 * LANES)),
                  full((LANES, w)), full((LANES, w)), full((2 * LANES, w)),
                  full((1, w)), full((1, w)), full((1, w)), full((1, w)), full((1, w))],
        out_specs=[pl.BlockSpec((tm, w), lambda i: (i, 0))] * 8,
        out_shape=[out] * 8,
        compiler_params=_params(("parallel",), nbytes),
        name="rwkv_prep",
    )(y, y, y, y, y, y, y, y, y, y, mu_r, mu_k, mu_v, mu_wa, mu_g, wup, aup, gup, w0, a0, k_k, k_a, r_k)


RW_CHUNK = 64
RW_PACK = MXU_DIM // RW_N
RW_NEUMANN = 8


def _rwkv_kernel(r_ref, lw_ref, k_ref, v_ref, kk_ref, a_ref, o_ref, s_ref):
    c, g4 = RW_CHUNK, MXU_DIM

    @pl.when(pl.program_id(0) == 0)
    def _():
        s_ref[...] = jnp.zeros_like(s_ref)

    ti = lax.broadcasted_iota(jnp.int32, (c, g4), 0)
    ci = lax.broadcasted_iota(jnp.int32, (c, g4), 1) % c
    strict = ci < ti
    incl = ci <= ti
    lane_head = lax.broadcasted_iota(jnp.int32, (1, g4), 1) // RW_N
    head_masks = [lane_head == h for h in range(RW_PACK)]
    bd_mask = (lax.broadcasted_iota(jnp.int32, (g4, g4), 0) // RW_N) == (lax.broadcasted_iota(jnp.int32, (g4, g4), 1) // RW_N)
    tr = lax.broadcasted_iota(jnp.int32, (c, c), 0)
    tc = lax.broadcasted_iota(jnp.int32, (c, c), 1)
    tri = jnp.where(tc <= tr, 1.0, 0.0).astype(BF16)

    def stack(parts):
        return [jnp.concatenate([jnp.where(m, p, jnp.zeros_like(p)) for m in head_masks], axis=0) for p in parts]

    groups = range(BRANCH // g4)
    lanes = [slice(grp * g4, (grp + 1) * g4) for grp in groups]

    def each(fn, *cols):
        return [fn(*args) for args in zip(*cols)]

    def prod(a, b_parts, dims=_NN, full=True):
        rows = a.shape[0]
        a_hi, a_lo = _split(a, 2)
        both = lax.dot_general(jnp.concatenate([a_hi, a_lo], axis=0), b_parts[0], dims, preferred_element_type=F32)
        out = both[:rows] + both[rows:]
        if full:
            out = out + lax.dot_general(a_hi, b_parts[1], dims, preferred_element_type=F32)
        return out

    def heads(b):
        return stack(_split(b, 2))

    lw_all = lw_ref[...]
    cum_all = _mm([tri], _split(lw_all, 3))
    lw = [lw_all[:, sl] for sl in lanes]
    cum = [cum_all[:, sl] for sl in lanes]
    cum_end = [x[c - 1:c, :] for x in cum]
    kk = [kk_ref[:, sl] for sl in lanes]
    kka = [x * a_ref[:, sl] for x, sl in zip(kk, lanes)]
    k = [k_ref[:, sl] for sl in lanes]
    v = [v_ref[:, sl] for sl in lanes]
    g_inv = each(lambda x: jnp.exp(-x), cum)
    tail = each(lambda e, x: jnp.exp(e - x), cum_end, cum)
    alp = each(lambda x, cm, l: x * jnp.exp(cm - l), kk, cum, lw)
    rho = each(lambda sl, cm: r_ref[:, sl] * jnp.exp(cm), lanes, cum)
    ar = each(lambda x, y: jnp.concatenate([x, y], axis=0), alp, rho)
    pb = each(lambda x, y, gi: prod(x, heads(y * gi), _NT), ar, kka, g_inv)
    pk = each(lambda x, y, gi: prod(x, heads(y * gi), _NT), ar, k, g_inv)
    l_b = each(lambda p: jnp.where(strict, p[:c], 0.0), pb)
    m_b = each(lambda p: jnp.where(incl, p[c:], 0.0), pb)
    lm_k = each(lambda p: jnp.concatenate([jnp.where(strict, p[:c], 0.0), jnp.where(incl, p[c:], 0.0)], axis=0), pk)

    nb = RW_NEUMANN
    eye = jnp.where(ci == ti, 1.0, 0.0)
    l1 = each(lambda ll: jnp.where((ci // nb) == (ti // nb), ll, 0.0), l_b)
    l2 = each(lambda ll: prod(ll, heads(ll)), l1)
    l34 = each(lambda ll, sq: prod(jnp.concatenate([ll, sq], axis=0), heads(sq)), l1, l2)
    a1 = each(lambda ll, sq, p: eye - ll + sq - p[:c], l1, l2, l34)
    x = each(lambda aa, p: aa + prod(aa, heads(p[c:])), a1, l34)
    m = nb
    while m < c:
        lower = ((ti // m) % 2 == 1) & ((ci // m) == (ti // m) - 1)
        y = each(lambda xx, ll: prod(xx, heads(jnp.where(lower, ll, 0.0))), x, l_b)
        x = each(lambda xx, yy: xx - prod(yy, heads(xx)), x, y)
        m *= 2

    st = [s_ref[grp] for grp in groups]
    ars = each(lambda a, s: prod(a, _split(s, 2), _NT), ar, st)
    lmv = each(lambda lm, vv: prod(lm, heads(vv)), lm_k, v)
    u = each(lambda xx, a, p: prod(xx, heads(a[:c] + p[:c])), x, ars, lmv)
    o = each(lambda a, p, mb, uu: a[c:] + p[c:] - prod(mb, heads(uu), full=False), ars, lmv, m_b, u)
    vu_t = each(lambda vv, uu: jnp.concatenate([vv, -uu], axis=0).T, v, u)
    kb = each(lambda kx, ka, t: jnp.concatenate([kx * t, ka * t], axis=0), k, kka, tail)
    st_new = each(lambda s, e, a, b: s * jnp.exp(e) + prod(a, _split(b, 2)), st, cum_end, vu_t, kb)
    for grp in groups:
        s_ref[grp] = jnp.where(bd_mask, st_new[grp], 0.0)
        o_ref[:, lanes[grp]] = o[grp]


def _rwkv_recurrence(r, lw, k, v, kk, a):
    s, w = r.shape
    c = RW_CHUNK
    spec = pl.BlockSpec((c, w), lambda i: (i, 0))
    nbytes = 2 * 7 * c * w * 4 + (w // MXU_DIM) * MXU_DIM * MXU_DIM * 4 + 64 * MXU_DIM * MXU_DIM * 4
    return pl.pallas_call(
        _rwkv_kernel,
        grid=(s // c,),
        in_specs=[spec] * 6,
        out_specs=spec,
        out_shape=jax.ShapeDtypeStruct((s, w), F32),
        scratch_shapes=[pltpu.VMEM((w // MXU_DIM, MXU_DIM, MXU_DIM), F32)],
        compiler_params=_params(("arbitrary",), nbytes),
        name="rwkv_recurrence",
    )(r, lw, k, v, kk, a)


def _rwkv_post_kernel(wkv_ref, g_ref, bonus_ref, lnw_ref, lnb_ref, o_ref):
    ones_bd = _head_ones()
    x = wkv_ref[...]
    d = x - _head_sum(x, ones_bd, 3) * (1.0 / RW_N)
    var = _head_sum(d * d, ones_bd, 3) * (1.0 / RW_N)
    y = d * lax.rsqrt(var + RW_LN_EPS) * lnw_ref[...] + lnb_ref[...]
    o_ref[...] = ((y + bonus_ref[...]) * g_ref[...]).astype(o_ref.dtype)


def _rwkv_post(wkv, g, bonus, ln_w, ln_b, tm=512):
    s, w = wkv.shape
    tm = min(tm, s)
    blk = pl.BlockSpec((tm, w), lambda i: (i, 0))
    vec = pl.BlockSpec((1, w), lambda i: (0, 0))
    nbytes = 2 * 4 * tm * w * 4 + 8 * tm * w * 4
    return pl.pallas_call(
        _rwkv_post_kernel,
        grid=(s // tm,),
        in_specs=[blk, blk, blk, vec, vec],
        out_specs=blk,
        out_shape=jax.ShapeDtypeStruct((s, w), BF16),
        compiler_params=_params(("parallel",), nbytes),
        name="rwkv_post",
    )(wkv, g, bonus, ln_w, ln_b)


def _merge_kernel(oa_ref, ob_ref, oc_ref, wb_ref, ga_ref, gb_ref, gc_ref, o_ref):
    acc = None
    for o_g, gate, idx in ((oa_ref, ga_ref, 0), (ob_ref, gb_ref, 1), (oc_ref, gc_ref, 2)):
        u = jnp.dot(o_g[...], wb_ref[idx], preferred_element_type=F32)
        t = jax.nn.sigmoid(gate[...]) * u
        acc = t if acc is None else acc + t
    o_ref[...] = acc.astype(o_ref.dtype)


def _merge(oa, ob, oc, w_branch, y, tm=512, tn=512):
    s, w = oa.shape
    d = w_branch.shape[2]
    tm = min(tm, s)
    branch = pl.BlockSpec((tm, w), lambda i, j: (i, 0))

    def gate(gi):
        return pl.BlockSpec((tm, tn), lambda i, j: (i, (C_GATES + gi * d) // tn + j))

    nbytes = 2 * 3 * tm * w * 2 + 2 * 3 * w * tn * 2 + 2 * 4 * tm * tn * 4 + 4 * tm * tn * 4
    return pl.pallas_call(
        _merge_kernel,
        grid=(s // tm, d // tn),
        in_specs=[branch, branch, branch, pl.BlockSpec((3, w, tn), lambda i, j: (0, 0, j)),
                  gate(0), gate(1), gate(2)],
        out_specs=pl.BlockSpec((tm, tn), lambda i, j: (i, j)),
        out_shape=jax.ShapeDtypeStruct((s, d), BF16),
        compiler_params=_params(("parallel", "arbitrary"), nbytes),
        name="merge",
    )(oa, ob, oc, w_branch, y, y, y)


def _out_proj_kernel(m_ref, w_ref, x_ref, g_ref, o_ref):
    mix = jnp.dot(m_ref[...], w_ref[...], preferred_element_type=F32)
    o_ref[...] = x_ref[...] + _rms(mix, g_ref[...])


def _out_proj(merged, w_out, x, gain, tm=512):
    s, d = x.shape
    tm = min(tm, s)
    nbytes = 2 * tm * d * 2 + 2 * d * d * 2 + 4 * tm * d * 4 + 2 * tm * d * 4
    return pl.pallas_call(
        _out_proj_kernel,
        grid=(s // tm,),
        in_specs=[pl.BlockSpec((tm, d), lambda i: (i, 0)), pl.BlockSpec((d, d), lambda i: (0, 0)),
                  pl.BlockSpec((tm, d), lambda i: (i, 0)), pl.BlockSpec((1, d), lambda i: (0, 0))],
        out_specs=pl.BlockSpec((tm, d), lambda i: (i, 0)),
        out_shape=jax.ShapeDtypeStruct((s, d), F32),
        compiler_params=_params(("parallel",), nbytes),
        name="out_proj",
    )(merged, w_out, x, gain)


def _pad_cols(w, width):
    return jnp.pad(w, ((0, 0), (0, width - w.shape[1])))


def _pad_rows(w, height, offset=0):
    return jnp.pad(w, ((offset, height - offset - w.shape[0]), (0, 0)))


def _pack_w_in(w):
    b = BRANCH
    gla0 = 3 * b
    alpha0 = gla0 + 2 * GLA_HEADS * GLA_DK + 2 * b
    rw0 = alpha0 + GLA_RANK
    wa0 = rw0 + 3 * b
    gl0 = wa0 + 2 * RW_RANK
    gates0 = gl0 + RW_GATE_RANK
    w_sb = jnp.concatenate([w[:, :b] * (SB_HEAD_DIM ** -0.5), w[:, b:3 * b]], axis=1).astype(BF16)
    w_rest = jnp.concatenate([
        w[:, gla0:alpha0], w[:, rw0:wa0], w[:, gates0:],
        _pad_cols(w[:, alpha0:rw0], LANES), w[:, wa0:gl0], _pad_cols(w[:, gl0:gates0], 2 * LANES)],
        axis=1).astype(BF16)
    return w_sb, w_rest


def _token_mixing(x, gain, w_in, gla_gate_up, gla_gate_bias, gla_norm, rwkv_mu, rwkv_w_up, rwkv_w0,
                  rwkv_a_up, rwkv_a0, rwkv_g_up, rwkv_k_k, rwkv_k_a, rwkv_r_k, rwkv_ln_w, rwkv_ln_b,
                  w_branch, w_out, gain_post):
    b = BRANCH
    w_sb, w_rest = _pack_w_in(w_in)
    qkv = _norm_matmul(x, gain, w_sb, BF16)
    y = _norm_matmul(x, gain, w_rest, F32)

    o_a = _sb_attention(qkv)

    o_b = _gla(y, _pad_rows(gla_gate_up, LANES).astype(BF16), gla_gate_bias[None, :], gla_norm[None, :])

    mu = rwkv_mu
    row = lambda t: t[None, :]
    prep = _rwkv_prep(
        y, row(mu[:b]), row(mu[b:2 * b]), row(mu[2 * b:3 * b]), row(mu[3 * b:3 * b + 2 * RW_RANK]),
        _pad_cols(row(mu[3 * b + 2 * RW_RANK:]), 2 * LANES),
        _pad_rows(rwkv_w_up, LANES).astype(BF16), _pad_rows(rwkv_a_up, LANES, RW_RANK).astype(BF16),
        _pad_rows(rwkv_g_up, 2 * LANES).astype(BF16),
        row(rwkv_w0), row(rwkv_a0), row(rwkv_k_k), row(rwkv_k_a), row(rwkv_r_k.reshape(-1)))
    r, lw, k, v, kk, a, g, bonus = prep
    wkv = _rwkv_recurrence(r, lw, k, v, kk, a)
    o_c = _rwkv_post(wkv, g, bonus, row(rwkv_ln_w), row(rwkv_ln_b))

    merged = _merge(o_a, o_b, o_c, w_branch.astype(BF16), y)
    return _out_proj(merged, w_out.astype(BF16), x, gain_post)


def kernel(x, norm_pre, norm_post, ffn_in, ffn_out, w_in, gla_gate_up, gla_gate_bias, gla_norm, rwkv_mu,
           rwkv_w_up, rwkv_w0, rwkv_a_up, rwkv_a0, rwkv_g_up, rwkv_k_k, rwkv_k_a, rwkv_r_k, rwkv_ln_w,
           rwkv_ln_b, w_branch, w_out):
    batch, seq, d = x.shape
    assert batch == 1 and d == D_MODEL
    h = x[0]
    for l in range(norm_pre.shape[0]):
        h = _ffn(h, norm_pre[l, 0][None, :], norm_post[l, 0][None, :],
                 ffn_in[l, 0].astype(BF16), ffn_out[l, 0].astype(BF16))
        h = _token_mixing(h, norm_pre[l, 1][None, :], w_in[l], gla_gate_up[l], gla_gate_bias[l], gla_norm[l],
                          rwkv_mu[l], rwkv_w_up[l], rwkv_w0[l], rwkv_a_up[l], rwkv_a0[l], rwkv_g_up[l],
                          rwkv_k_k[l], rwkv_k_a[l], rwkv_r_k[l], rwkv_ln_w[l], rwkv_ln_b[l],
                          w_branch[l], w_out[l], norm_post[l, 1][None, :])
        h = _ffn(h, norm_pre[l, 2][None, :], norm_post[l, 2][None, :],
                 ffn_in[l, 1].astype(BF16), ffn_out[l, 1].astype(BF16))
    return h[None]
```

```python
import functools
import math

import jax
import jax.numpy as jnp
from jax import lax
from jax.experimental import pallas as pl
from jax.experimental.pallas import tpu as pltpu

F32 = jnp.float32
BF16 = jnp.bfloat16

D_MODEL = 2048
BRANCH = D_MODEL // 2
D_FF = 256 * math.ceil(8 * D_MODEL / 3 / 256)
FFN_SCALE = 0.5
NORM_EPS = 1e-6

SB_HEAD_DIM = 64
GLA_HEADS = 4
GLA_DV = BRANCH // GLA_HEADS
GLA_DK = GLA_DV // 2
GLA_RANK = 16
GLA_TAU = 16.0
GLA_CHUNK = 64
RW_N = 64
RW_RANK = 64
RW_GATE_RANK = 160
RW_LN_EPS = 64e-5

LANES = 128
MXU_DIM = 256
VMEM_LIMIT_CAP = 60000 * 1024

C_GLA_Q = 0
C_GLA_K = 512
C_GLA_V = 1024
C_GLA_R = 2048
C_RW_R = 3072
C_RW_K = 4096
C_RW_V = 5120
C_GATES = 6144
C_ALPHA = 12288
C_WA = 12416
C_GLOW = 12544
N_REST = 12800
N_SB = 3 * BRANCH


def _vmem_limit(nbytes):
    return int(min(max(2 * nbytes, 32 * 1024 * 1024), VMEM_LIMIT_CAP))


def _params(sem, nbytes):
    return pltpu.CompilerParams(dimension_semantics=sem, vmem_limit_bytes=_vmem_limit(nbytes))


def _rms(x, gain):
    ms = jnp.mean(x * x, axis=-1, keepdims=True)
    return x * lax.rsqrt(ms + NORM_EPS) * gain


def _softplus(x):
    return jnp.maximum(x, 0.0) + jnp.log(1.0 + jnp.exp(-jnp.abs(x)))


def _split(x, n):
    parts = []
    for _ in range(n - 1):
        p = x.astype(BF16)
        parts.append(p)
        x = x - p.astype(F32)
    parts.append(x.astype(BF16))
    return parts


_NN = (((1,), (0,)), ((), ()))
_NT = (((1,), (1,)), ((), ()))


def _mm(a_parts, b_parts, dims=_NN, order=3):
    acc = None
    for i, a in enumerate(a_parts):
        for j, b in enumerate(b_parts):
            if i + j < order:
                t = lax.dot_general(a, b, dims, preferred_element_type=F32)
                acc = t if acc is None else acc + t
    return acc


def _ffn_kernel(x_ref, gpre_ref, gpost_ref, wg_ref, wu_ref, wo_ref, o_ref, h_ref, acc_ref):
    j = pl.program_id(1)

    @pl.when(j == 0)
    def _():
        h_ref[...] = _rms(x_ref[...], gpre_ref[...]).astype(BF16)
        acc_ref[...] = jnp.zeros_like(acc_ref)

    h = h_ref[...]
    g = jnp.dot(h, wg_ref[...], preferred_element_type=F32)
    u = jnp.dot(h, wu_ref[...], preferred_element_type=F32)
    a = (g * jax.nn.sigmoid(g) * u).astype(BF16)
    acc_ref[...] += jnp.dot(a, wo_ref[...], preferred_element_type=F32)

    @pl.when(j == pl.num_programs(1) - 1)
    def _():
        o_ref[...] = x_ref[...] + FFN_SCALE * _rms(acc_ref[...], gpost_ref[...])


def _ffn(x, gpre, gpost, w_in, w_out, tm=512, tf=512):
    s, d = x.shape
    f = w_out.shape[0]
    tm = min(tm, s)
    nj = f // tf
    nbytes = 2 * (2 * tm * d * 4) + tm * d * 6 + 2 * (3 * d * tf * 2) + 4 * tm * tf * 4
    return pl.pallas_call(
        _ffn_kernel,
        grid=(s // tm, nj),
        in_specs=[
            pl.BlockSpec((tm, d), lambda i, j: (i, 0)),
            pl.BlockSpec((1, d), lambda i, j: (0, 0)),
            pl.BlockSpec((1, d), lambda i, j: (0, 0)),
            pl.BlockSpec((d, tf), lambda i, j: (0, j)),
            pl.BlockSpec((d, tf), lambda i, j: (0, j + nj)),
            pl.BlockSpec((tf, d), lambda i, j: (j, 0)),
        ],
        out_specs=pl.BlockSpec((tm, d), lambda i, j: (i, 0)),
        out_shape=jax.ShapeDtypeStruct((s, d), F32),
        scratch_shapes=[pltpu.VMEM((tm, d), BF16), pltpu.VMEM((tm, d), F32)],
        compiler_params=_params(("parallel", "arbitrary"), nbytes),
        name="ffn",
    )(x, gpre, gpost, w_in, w_in, w_out)


def _norm_matmul_kernel(x_ref, g_ref, w_ref, o_ref, h_ref):
    @pl.when(pl.program_id(1) == 0)
    def _():
        h_ref[...] = _rms(x_ref[...], g_ref[...]).astype(BF16)

    o_ref[...] = jnp.dot(h_ref[...], w_ref[...], preferred_element_type=F32).astype(o_ref.dtype)


def _norm_matmul(x, gain, w, out_dtype, tm=1024, tn=512):
    s, d = x.shape
    n = w.shape[1]
    tm = min(tm, s)
    nbytes = 2 * tm * d * 4 + tm * d * 2 + 2 * d * tn * 2 + 2 * tm * tn * 4
    return pl.pallas_call(
        _norm_matmul_kernel,
        grid=(s // tm, n // tn),
        in_specs=[
            pl.BlockSpec((tm, d), lambda i, j: (i, 0)),
            pl.BlockSpec((1, d), lambda i, j: (0, 0)),
            pl.BlockSpec((d, tn), lambda i, j: (0, j)),
        ],
        out_specs=pl.BlockSpec((tm, tn), lambda i, j: (i, j)),
        out_shape=jax.ShapeDtypeStruct((s, n), out_dtype),
        scratch_shapes=[pltpu.VMEM((tm, d), BF16)],
        compiler_params=_params(("parallel", "arbitrary"), nbytes),
        name="norm_matmul",
    )(x, gain, w)


SB_TILE = 256
SB_SUB = 4
SB_EXP_CLAMP = 40.0
SB_LOG_ZERO = -110.0
SB_NO_KEYS = -1e30


def _sb_kernel(q_ref, k_ref, v_ref, o_ref):
    t, nsub = SB_TILE, SB_SUB
    first = nsub * pl.program_id(1)
    left = lax.broadcasted_iota(jnp.int32, (1, LANES), 1) < SB_HEAD_DIM
    q_heads = []
    for s in range(nsub):
        q = q_ref[s * t:(s + 1) * t, :]
        qz = jnp.zeros_like(q)
        q_heads.append((jnp.where(left, q, qz), jnp.where(left, qz, q)))
    row = lax.broadcasted_iota(jnp.int32, (t, t), 0)
    col = lax.broadcasted_iota(jnp.int32, (t, t), 1)
    neg_tri = jnp.where(row >= col, -1.0, 0.0).astype(BF16)
    neg_tri2 = jnp.concatenate([neg_tri, neg_tri], axis=0)
    causal = col < row
    chains = [(s, hh) for s in range(nsub) for hh in range(2)]

    def step(back, diagonal, accs, cs):
        ks, vs = [], []
        for s in range(nsub):
            kb = first + s - back
            if not diagonal:
                cs = [jnp.where(kb >= 0, c, SB_NO_KEYS) if ch[0] == s else c for c, ch in zip(cs, chains)]
                kb = jnp.maximum(kb, 0)
            start = pl.multiple_of(kb * t, t)
            ks.append(k_ref[pl.ds(start, t), :])
            v = v_ref[pl.ds(start, t), :]
            vz = jnp.zeros_like(v)
            vs.append((jnp.where(left, v, vz), jnp.where(left, vz, v)))
        z = [lax.dot_general(q_heads[s][hh], ks[s], _NT, preferred_element_type=F32) for s, hh in chains]
        sp = [jnp.maximum(jnp.log(1.0 + jnp.exp(jnp.minimum(x, SB_EXP_CLAMP))), x) for x in z]
        if diagonal:
            sp = [jnp.where(causal, x, 0.0) for x in sp]
        cum = [jnp.dot(jnp.concatenate(_split(x, 2), axis=1), neg_tri2, preferred_element_type=F32) for x in sp]
        w = [jnp.exp(zz + cc + c) for zz, cc, c in zip(z, cum, cs)]
        if diagonal:
            w = [jnp.where(causal, x, 0.0) for x in w]
        pv = [jnp.dot(x.astype(BF16), vs[s][hh], preferred_element_type=F32) for x, (s, hh) in zip(w, chains)]
        accs = [accs[s] + pv[2 * s] + pv[2 * s + 1] for s in range(nsub)]
        cs = [c + cc[:, :1] for c, cc in zip(cs, cum)]
        live = jnp.max(functools.reduce(jnp.maximum, cs)) >= SB_LOG_ZERO
        return accs, cs, live.astype(jnp.int32)

    accs = [jnp.zeros((t, LANES), F32) for _ in range(nsub)]
    cs = [jnp.zeros((t, 1), F32) for _ in chains]
    accs, cs, live = step(0, True, accs, cs)

    def more(carry):
        return jnp.logical_and(carry[0] < first + nsub, carry[1] > 0)

    def walk(carry):
        accs, cs, live = step(carry[0], False, list(carry[2]), list(carry[3]))
        return carry[0] + 1, live, tuple(accs), tuple(cs)

    carry = lax.while_loop(more, walk, (jnp.int32(1), live, tuple(accs), tuple(cs)))
    for s in range(nsub):
        o_ref[s * t:(s + 1) * t, :] = carry[2][s].astype(o_ref.dtype)


def _sb_attention(qkv):
    s = qkv.shape[0]
    t = SB_TILE
    tq = SB_SUB * t
    npair = BRANCH // LANES
    nbytes = 2 * 2 * s * LANES * 2 + 16 * SB_SUB * t * t * 4
    return pl.pallas_call(
        _sb_kernel,
        grid=(npair, s // tq),
        in_specs=[
            pl.BlockSpec((tq, LANES), lambda p, i: (i, p)),
            pl.BlockSpec((s, LANES), lambda p, i: (0, npair + p)),
            pl.BlockSpec((s, LANES), lambda p, i: (0, 2 * npair + p)),
        ],
        out_specs=pl.BlockSpec((tq, LANES), lambda p, i: (i, p)),
        out_shape=jax.ShapeDtypeStruct((s, BRANCH), BF16),
        compiler_params=_params(("parallel", "arbitrary"), nbytes),
        name="sb_attention",
    )(qkv, qkv, qkv)


GLA_TILE = 256


def _gla_kernel(q_ref, k_ref, v_ref, r_ref, al_ref, gup_ref, gb_ref, gn_ref, o_ref, st_ref):
    t, c, dk, dv = GLA_TILE, GLA_CHUNK, GLA_DK, GLA_DV

    @pl.when(pl.program_id(0) == 0)
    def _():
        st_ref[...] = jnp.zeros_like(st_ref)

    x = jnp.dot(al_ref[...].astype(BF16), gup_ref[...], preferred_element_type=F32) + gb_ref[...]
    g = -_softplus(-x) * (1.0 / GLA_TAU)
    row = lax.broadcasted_iota(jnp.int32, (t, t), 0)
    col = lax.broadcasted_iota(jnp.int32, (t, t), 1)
    same = (row // c) == (col // c)
    tril = same & (col <= row)
    g_parts = _split(g, 3)
    b_all = _mm([jnp.where(tril, 1.0, 0.0).astype(BF16)], g_parts)
    last_all = _mm([jnp.where(same, 1.0, 0.0).astype(BF16)], g_parts)
    hs = range(GLA_HEADS)
    kcols = [slice(h * dk, (h + 1) * dk) for h in hs]
    vcols = [slice(h * dv, (h + 1) * dv) for h in hs]
    b = [b_all[:, sl] for sl in kcols]
    b_last = [last_all[:, sl] for sl in kcols]
    k = [k_ref[:, sl] for sl in kcols]
    v = [v_ref[:, sl].astype(BF16) for sl in vcols]
    q_dec = [(q_ref[:, sl] * (dk ** -0.5) * jnp.exp(bb)).astype(BF16) for sl, bb in zip(kcols, b)]
    k_dec = [(kk * jnp.exp(-bb)).astype(BF16) for kk, bb in zip(k, b)]
    k_tail = [(kk * jnp.exp(bl - bb)).astype(BF16) for kk, bl, bb in zip(k, b_last, b)]
    chunk_decay = [jnp.exp(bl) for bl in b_last]
    scores = [lax.dot_general(qd, kd, _NT, preferred_element_type=F32) for qd, kd in zip(q_dec, k_dec)]
    scores = [jnp.where(tril, sc, 0.0).astype(BF16) for sc in scores]
    o_intra = [jnp.dot(sc, vv, preferred_element_type=F32) for sc, vv in zip(scores, v)]
    v_t = [vv.astype(F32).T.astype(BF16) for vv in v]
    st = [st_ref[h] for h in hs]
    outs = [[] for _ in hs]
    for n in range(t // c):
        rows = slice(n * c, (n + 1) * c)
        o_inter = [lax.dot_general(qd[rows], s.astype(BF16), _NT, preferred_element_type=F32) for qd, s in zip(q_dec, st)]
        for h in hs:
            outs[h].append(o_intra[h][rows] + o_inter[h])
        kv = [jnp.dot(vt[:, rows], kt[rows], preferred_element_type=F32) for vt, kt in zip(v_t, k_tail)]
        st = [s * cd[n * c:n * c + 1, :] + x for s, cd, x in zip(st, chunk_decay, kv)]
    for h in hs:
        st_ref[h] = st[h]
        o = jnp.concatenate(outs[h], axis=0)
        o = o * lax.rsqrt(jnp.mean(o * o, axis=-1, keepdims=True) + NORM_EPS)
        r = r_ref[:, vcols[h]]
        o_ref[:, vcols[h]] = (o * gn_ref[:, vcols[h]] * (r * jax.nn.sigmoid(r))).astype(o_ref.dtype)


def _gla(y, gate_up, gate_bias, gla_norm):
    s = y.shape[0]
    t = min(GLA_TILE, s)
    kw, vw = GLA_HEADS * GLA_DK, GLA_HEADS * GLA_DV
    nbytes = 2 * t * (2 * kw + 3 * vw + LANES) * 4 + GLA_HEADS * 12 * t * t * 4
    return pl.pallas_call(
        _gla_kernel,
        grid=(s // t,),
        in_specs=[
            pl.BlockSpec((t, kw), lambda i: (i, C_GLA_Q // kw)),
            pl.BlockSpec((t, kw), lambda i: (i, C_GLA_K // kw)),
            pl.BlockSpec((t, vw), lambda i: (i, C_GLA_V // vw)),
            pl.BlockSpec((t, vw), lambda i: (i, C_GLA_R // vw)),
            pl.BlockSpec((t, LANES), lambda i: (i, C_ALPHA // LANES)),
            pl.BlockSpec((LANES, kw), lambda i: (0, 0)),
            pl.BlockSpec((1, kw), lambda i: (0, 0)),
            pl.BlockSpec((1, vw), lambda i: (0, 0)),
        ],
        out_specs=pl.BlockSpec((t, vw), lambda i: (i, 0)),
        out_shape=jax.ShapeDtypeStruct((s, BRANCH), BF16),
        scratch_shapes=[pltpu.VMEM((GLA_HEADS, GLA_DV, GLA_DK), F32)],
        compiler_params=_params(("arbitrary",), nbytes),
        name="gla",
    )(y, y, y, y, y, gate_up, gate_bias, gla_norm)


def _head_ones():
    r = lax.broadcasted_iota(jnp.int32, (MXU_DIM, MXU_DIM), 0) // RW_N
    c = lax.broadcasted_iota(jnp.int32, (MXU_DIM, MXU_DIM), 1) // RW_N
    return jnp.where(r == c, 1.0, 0.0).astype(BF16)


def _head_sum(x, ones_bd, n_split):
    outs = []
    for i in range(x.shape[1] // MXU_DIM):
        outs.append(_mm(_split(x[:, i * MXU_DIM:(i + 1) * MXU_DIM], n_split), [ones_bd], order=n_split))
    return jnp.concatenate(outs, axis=1)


def _rwkv_prep_kernel(yr_ref, yk_ref, yv_ref, ywa_ref, yg_ref, pr_ref, pk_ref, pv_ref, pwa_ref, pg_ref,
                      mur_ref, muk_ref, muv_ref, muwa_ref, mug_ref, wup_ref, aup_ref, gup_ref,
                      w0_ref, a0_ref, kk_ref, ka_ref, rk_ref,
                      r_out, lw_out, k_out, v_out, kk_out, a_out, g_out, bonus_out):
    first = pl.program_id(0) == 0

    def shifted(y_ref, p_ref, mu_ref):
        y = y_ref[...]
        prev_row = jnp.where(first, 0.0, p_ref[7:8, :])
        row = lax.broadcasted_iota(jnp.int32, y.shape, 0)
        y_prev = jnp.where(row == 0, prev_row, pltpu.roll(y, 1, axis=0))
        return y + (y_prev - y) * mu_ref[...]

    r = shifted(yr_ref, pr_ref, mur_ref)
    k = shifted(yk_ref, pk_ref, muk_ref)
    v = shifted(yv_ref, pv_ref, muv_ref)
    wa = shifted(ywa_ref, pwa_ref, muwa_ref)
    gl = shifted(yg_ref, pg_ref, mug_ref)
    w_pre = w0_ref[...] + jnp.dot(jnp.tanh(wa).astype(BF16), wup_ref[...], preferred_element_type=F32)
    lw = -jnp.exp(-_softplus(-w_pre) - 0.5)
    a = jax.nn.sigmoid(a0_ref[...] + jnp.dot(wa.astype(BF16), aup_ref[...], preferred_element_type=F32))
    g = jnp.dot(jax.nn.sigmoid(gl).astype(BF16), gup_ref[...], preferred_element_type=F32)
    ones_bd = _head_ones()
    kk = k * kk_ref[...]
    kk = kk / jnp.maximum(jnp.sqrt(_head_sum(kk * kk, ones_bd, 3)), 1e-12)
    k = k * (1.0 + (a - 1.0) * ka_ref[...])
    bonus = _head_sum(r * k * rk_ref[...], ones_bd, 3) * v
    r_out[...] = r
    lw_out[...] = lw
    k_out[...] = k
    v_out[...] = v
    kk_out[...] = kk
    a_out[...] = a
    g_out[...] = g
    bonus_out[...] = bonus


def _rwkv_prep(y, mu_r, mu_k, mu_v, mu_wa, mu_g, wup, aup, gup, w0, a0, k_k, k_a, r_k, tm=512):
    s = y.shape[0]
    tm = min(tm, s)
    w = BRANCH

    def cur(width, col):
        return pl.BlockSpec((tm, width), lambda i: (i, col // width))

    def prev(width, col):
        return pl.BlockSpec((8, width), lambda i: (jnp.maximum(i * (tm // 8) - 1, 0), col // width))

    def full(shape):
        return pl.BlockSpec(shape, lambda i: (0, 0))

    nbytes = 2 * tm * (3 * w + 384) * 4 + 2 * 8 * tm * w * 4 + 12 * tm * w * 4
    out = jax.ShapeDtypeStruct((s, w), F32)
    return pl.pallas_call(
        _rwkv_prep_kernel,
        grid=(s // tm,),
        in_specs=[cur(w, C_RW_R), cur(w, C_RW_K), cur(w, C_RW_V), cur(LANES, C_WA), cur(2 * LANES, C_GLOW),
                  prev(w, C_RW_R), prev(w, C_RW_K), prev(w, C_RW_V), prev(LANES, C_WA), prev(2 * LANES, C_GLOW),
                  full((1, w)), full((1, w)), full((1, w)), full((1, LANES)), full((1, 2 * LANES)),
                  full((LANES, w)), full((LANES, w)), full((2 * LANES, w)),
                  full((1, w)), full((1, w)), full((1, w)), full((1, w)), full((1, w))],
        out_specs=[pl.BlockSpec((tm, w), lambda i: (i, 0))] * 8,
        out_shape=[out] * 8,
        compiler_params=_params(("parallel",), nbytes),
        name="rwkv_prep",
    )(y, y, y, y, y, y, y, y, y, y, mu_r, mu_k, mu_v, mu_wa, mu_g, wup, aup, gup, w0, a0, k_k, k_a, r_k)


RW_CHUNK = 64
RW_PACK = MXU_DIM // RW_N
RW_NEUMANN = 8
RW_STEP_CHUNKS = 2


def _rwkv_kernel(r_ref, lw_ref, k_ref, v_ref, kk_ref, a_ref, o_ref, s_ref):
    c, g4 = RW_CHUNK, MXU_DIM

    @pl.when(pl.program_id(0) == 0)
    def _():
        s_ref[...] = jnp.zeros_like(s_ref)

    ti = lax.broadcasted_iota(jnp.int32, (c, g4), 0)
    ci = lax.broadcasted_iota(jnp.int32, (c, g4), 1) % c
    strict = ci < ti
    incl = ci <= ti
    lane_head = lax.broadcasted_iota(jnp.int32, (1, g4), 1) // RW_N
    head_masks = [lane_head == h for h in range(RW_PACK)]
    bd_mask = (lax.broadcasted_iota(jnp.int32, (g4, g4), 0) // RW_N) == (lax.broadcasted_iota(jnp.int32, (g4, g4), 1) // RW_N)
    tr = lax.broadcasted_iota(jnp.int32, (RW_STEP_CHUNKS * c, RW_STEP_CHUNKS * c), 0)
    tc = lax.broadcasted_iota(jnp.int32, (RW_STEP_CHUNKS * c, RW_STEP_CHUNKS * c), 1)
    tri = jnp.where((tc <= tr) & (tc // c == tr // c), 1.0, 0.0).astype(BF16)

    def heads(b):
        p = b.astype(BF16)
        return jnp.concatenate([jnp.where(m, p, jnp.zeros_like(p)) for m in head_masks], axis=0)

    groups = range(BRANCH // g4)
    lanes = [slice(grp * g4, (grp + 1) * g4) for grp in groups]
    rows = [slice(n * c, (n + 1) * c) for n in range(RW_STEP_CHUNKS)]
    units = [(rw, sl) for rw in rows for sl in lanes]

    def each(fn, *cols):
        return [fn(*args) for args in zip(*cols)]

    def prod(a, b, dims=_NN):
        m = a.shape[0]
        both = lax.dot_general(jnp.concatenate(_split(a, 2), axis=0), b, dims, preferred_element_type=F32)
        return both[:m] + both[m:]

    lw_all = lw_ref[...]
    cum_all = _mm([tri], _split(lw_all, 3))
    lw = [lw_all[rw, sl] for rw, sl in units]
    cum = [cum_all[rw, sl] for rw, sl in units]
    cum_end = [x[c - 1:c, :] for x in cum]
    kk = [kk_ref[rw, sl] for rw, sl in units]
    kka = [x * a_ref[rw, sl] for x, (rw, sl) in zip(kk, units)]
    k = [k_ref[rw, sl] for rw, sl in units]
    v = [v_ref[rw, sl] for rw, sl in units]
    g_inv = each(lambda x: jnp.exp(-x), cum)
    tail = each(lambda e, x: jnp.exp(e - x), cum_end, cum)
    alp = each(lambda x, cm, l: x * jnp.exp(cm - l), kk, cum, lw)
    rho = each(lambda un, cm: r_ref[un[0], un[1]] * jnp.exp(cm), units, cum)
    ar = each(lambda x, y: jnp.concatenate([x, y], axis=0), alp, rho)
    pb = each(lambda x, y, gi: prod(x, heads(y * gi), _NT), ar, kka, g_inv)
    pk = each(lambda x, y, gi: prod(x, heads(y * gi), _NT), ar, k, g_inv)
    l_b = each(lambda p: jnp.where(strict, p[:c], 0.0), pb)
    m_b = each(lambda p: jnp.where(incl, p[c:], 0.0), pb)
    lm_k = each(lambda p: jnp.concatenate([jnp.where(strict, p[:c], 0.0), jnp.where(incl, p[c:], 0.0)], axis=0), pk)

    nb = RW_NEUMANN
    eye = jnp.where(ci == ti, 1.0, 0.0)
    l1 = each(lambda ll: jnp.where((ci // nb) == (ti // nb), ll, 0.0), l_b)
    l2 = each(lambda ll: prod(ll, heads(ll)), l1)
    l34 = each(lambda ll, sq: prod(jnp.concatenate([ll, sq], axis=0), heads(sq)), l1, l2)
    a1 = each(lambda ll, sq, p: eye - ll + sq - p[:c], l1, l2, l34)
    x = each(lambda aa, p: aa + prod(aa, heads(p[c:])), a1, l34)
    m = nb
    while m < c:
        lower = ((ti // m) % 2 == 1) & ((ci // m) == (ti // m) - 1)
        y = each(lambda xx, ll: prod(xx, heads(jnp.where(lower, ll, 0.0))), x, l_b)
        x = each(lambda xx, yy: xx - prod(yy, heads(xx)), x, y)
        m *= 2

    lmv = each(lambda lm, vv: prod(lm, heads(vv)), lm_k, v)
    kb = each(lambda kx, ka, t: jnp.concatenate([kx * t, ka * t], axis=0).astype(BF16), k, kka, tail)
    g_end = each(jnp.exp, cum_end)

    st = [s_ref[grp] for grp in groups]
    ng = len(lanes)
    for n, rw in enumerate(rows):
        mine = slice(n * ng, (n + 1) * ng)
        ars = each(lambda a, s: prod(a, s.astype(BF16), _NT), ar[mine], st)
        u = each(lambda xx, a, p: prod(xx, heads(a[:c] + p[:c])), x[mine], ars, lmv[mine])
        o = each(lambda a, p, mb, uu: a[c:] + p[c:] - prod(mb, heads(uu)), ars, lmv[mine], m_b[mine], u)
        vu_t = each(lambda vv, uu: jnp.concatenate([vv, -uu], axis=0).T, v[mine], u)
        st = each(lambda s, e, a, b: jnp.where(bd_mask, s * e + prod(a, b), 0.0), st, g_end[mine], vu_t, kb[mine])
        for grp in groups:
            o_ref[rw, lanes[grp]] = o[grp]
    for grp in groups:
        s_ref[grp] = st[grp]


def _rwkv_recurrence(r, lw, k, v, kk, a):
    s, w = r.shape
    c = RW_STEP_CHUNKS * RW_CHUNK
    spec = pl.BlockSpec((c, w), lambda i: (i, 0))
    nbytes = 2 * 7 * c * w * 4 + (w // MXU_DIM) * MXU_DIM * MXU_DIM * 4 + RW_STEP_CHUNKS * 64 * MXU_DIM * MXU_DIM * 4
    return pl.pallas_call(
        _rwkv_kernel,
        grid=(s // c,),
        in_specs=[spec] * 6,
        out_specs=spec,
        out_shape=jax.ShapeDtypeStruct((s, w), F32),
        scratch_shapes=[pltpu.VMEM((w // MXU_DIM, MXU_DIM, MXU_DIM), F32)],
        compiler_params=_params(("arbitrary",), nbytes),
        name="rwkv_recurrence",
    )(r, lw, k, v, kk, a)


def _rwkv_post_kernel(wkv_ref, g_ref, bonus_ref, lnw_ref, lnb_ref, o_ref):
    ones_bd = _head_ones()
    x = wkv_ref[...]
    d = x - _head_sum(x, ones_bd, 3) * (1.0 / RW_N)
    var = _head_sum(d * d, ones_bd, 3) * (1.0 / RW_N)
    y = d * lax.rsqrt(var + RW_LN_EPS) * lnw_ref[...] + lnb_ref[...]
    o_ref[...] = ((y + bonus_ref[...]) * g_ref[...]).astype(o_ref.dtype)


def _rwkv_post(wkv, g, bonus, ln_w, ln_b, tm=512):
    s, w = wkv.shape
    tm = min(tm, s)
    blk = pl.BlockSpec((tm, w), lambda i: (i, 0))
    vec = pl.BlockSpec((1, w), lambda i: (0, 0))
    nbytes = 2 * 4 * tm * w * 4 + 8 * tm * w * 4
    return pl.pallas_call(
        _rwkv_post_kernel,
        grid=(s // tm,),
        in_specs=[blk, blk, blk, vec, vec],
        out_specs=blk,
        out_shape=jax.ShapeDtypeStruct((s, w), BF16),
        compiler_params=_params(("parallel",), nbytes),
        name="rwkv_post",
    )(wkv, g, bonus, ln_w, ln_b)


def _merge_kernel(oa_ref, ob_ref, oc_ref, wb_ref, ga_ref, gb_ref, gc_ref, o_ref):
    acc = None
    for o_g, gate, idx in ((oa_ref, ga_ref, 0), (ob_ref, gb_ref, 1), (oc_ref, gc_ref, 2)):
        u = jnp.dot(o_g[...], wb_ref[idx], preferred_element_type=F32)
        t = jax.nn.sigmoid(gate[...]) * u
        acc = t if acc is None else acc + t
    o_ref[...] = acc.astype(o_ref.dtype)


def _merge(oa, ob, oc, w_branch, y, tm=512, tn=512):
    s, w = oa.shape
    d = w_branch.shape[2]
    tm = min(tm, s)
    branch = pl.BlockSpec((tm, w), lambda i, j: (i, 0))

    def gate(gi):
        return pl.BlockSpec((tm, tn), lambda i, j: (i, (C_GATES + gi * d) // tn + j))

    nbytes = 2 * 3 * tm * w * 2 + 2 * 3 * w * tn * 2 + 2 * 4 * tm * tn * 4 + 4 * tm * tn * 4
    return pl.pallas_call(
        _merge_kernel,
        grid=(s // tm, d // tn),
        in_specs=[branch, branch, branch, pl.BlockSpec((3, w, tn), lambda i, j: (0, 0, j)),
                  gate(0), gate(1), gate(2)],
        out_specs=pl.BlockSpec((tm, tn), lambda i, j: (i, j)),
        out_shape=jax.ShapeDtypeStruct((s, d), BF16),
        compiler_params=_params(("parallel", "arbitrary"), nbytes),
        name="merge",
    )(oa, ob, oc, w_branch, y, y, y)


def _out_proj_kernel(m_ref, w_ref, x_ref, g_ref, o_ref):
    mix = jnp.dot(m_ref[...], w_ref[...], preferred_element_type=F32)
    o_ref[...] = x_ref[...] + _rms(mix, g_ref[...])


def _out_proj(merged, w_out, x, gain, tm=512):
    s, d = x.shape
    tm = min(tm, s)
    nbytes = 2 * tm * d * 2 + 2 * d * d * 2 + 4 * tm * d * 4 + 2 * tm * d * 4
    return pl.pallas_call(
        _out_proj_kernel,
        grid=(s // tm,),
        in_specs=[pl.BlockSpec((tm, d), lambda i: (i, 0)), pl.BlockSpec((d, d), lambda i: (0, 0)),
                  pl.BlockSpec((tm, d), lambda i: (i, 0)), pl.BlockSpec((1, d), lambda i: (0, 0))],
        out_specs=pl.BlockSpec((tm, d), lambda i: (i, 0)),
        out_shape=jax.ShapeDtypeStruct((s, d), F32),
        compiler_params=_params(("parallel",), nbytes),
        name="out_proj",
    )(merged, w_out, x, gain)


def _pad_cols(w, width):
    return jnp.pad(w, ((0, 0), (0, width - w.shape[1])))


def _pad_rows(w, height, offset=0):
    return jnp.pad(w, ((offset, height - offset - w.shape[0]), (0, 0)))


def _pack_w_in(w):
    b = BRANCH
    gla0 = 3 * b
    alpha0 = gla0 + 2 * GLA_HEADS * GLA_DK + 2 * b
    rw0 = alpha0 + GLA_RANK
    wa0 = rw0 + 3 * b
    gl0 = wa0 + 2 * RW_RANK
    gates0 = gl0 + RW_GATE_RANK
    w_sb = jnp.concatenate([w[:, :b] * (SB_HEAD_DIM ** -0.5), w[:, b:3 * b]], axis=1).astype(BF16)
    w_rest = jnp.concatenate([
        w[:, gla0:alpha0], w[:, rw0:wa0], w[:, gates0:],
        _pad_cols(w[:, alpha0:rw0], LANES), w[:, wa0:gl0], _pad_cols(w[:, gl0:gates0], 2 * LANES)],
        axis=1).astype(BF16)
    return w_sb, w_rest


def _token_mixing(x, gain, w_in, gla_gate_up, gla_gate_bias, gla_norm, rwkv_mu, rwkv_w_up, rwkv_w0,
                  rwkv_a_up, rwkv_a0, rwkv_g_up, rwkv_k_k, rwkv_k_a, rwkv_r_k, rwkv_ln_w, rwkv_ln_b,
                  w_branch, w_out, gain_post):
    b = BRANCH
    w_sb, w_rest = _pack_w_in(w_in)
    qkv = _norm_matmul(x, gain, w_sb, BF16)
    y = _norm_matmul(x, gain, w_rest, F32)

    o_a = _sb_attention(qkv)

    o_b = _gla(y, _pad_rows(gla_gate_up, LANES).astype(BF16), gla_gate_bias[None, :], gla_norm[None, :])

    mu = rwkv_mu
    row = lambda t: t[None, :]
    prep = _rwkv_prep(
        y, row(mu[:b]), row(mu[b:2 * b]), row(mu[2 * b:3 * b]), row(mu[3 * b:3 * b + 2 * RW_RANK]),
        _pad_cols(row(mu[3 * b + 2 * RW_RANK:]), 2 * LANES),
        _pad_rows(rwkv_w_up, LANES).astype(BF16), _pad_rows(rwkv_a_up, LANES, RW_RANK).astype(BF16),
        _pad_rows(rwkv_g_up, 2 * LANES).astype(BF16),
        row(rwkv_w0), row(rwkv_a0), row(rwkv_k_k), row(rwkv_k_a), row(rwkv_r_k.reshape(-1)))
    r, lw, k, v, kk, a, g, bonus = prep
    wkv = _rwkv_recurrence(r, lw, k, v, kk, a)
    o_c = _rwkv_post(wkv, g, bonus, row(rwkv_ln_w), row(rwkv_ln_b))

    merged = _merge(o_a, o_b, o_c, w_branch.astype(BF16), y)
    return _out_proj(merged, w_out.astype(BF16), x, gain_post)


def kernel(x, norm_pre, norm_post, ffn_in, ffn_out, w_in, gla_gate_up, gla_gate_bias, gla_norm, rwkv_mu,
           rwkv_w_up, rwkv_w0, rwkv_a_up, rwkv_a0, rwkv_g_up, rwkv_k_k, rwkv_k_a, rwkv_r_k, rwkv_ln_w,
           rwkv_ln_b, w_branch, w_out):
    batch, seq, d = x.shape
    assert batch == 1 and d == D_MODEL
    h = x[0]
    w_in = w_in.astype(BF16)
    for l in range(norm_pre.shape[0]):
        h = _ffn(h, norm_pre[l, 0][None, :], norm_post[l, 0][None, :],
                 ffn_in[l, 0].astype(BF16), ffn_out[l, 0].astype(BF16))
        h = _token_mixing(h, norm_pre[l, 1][None, :], w_in[l], gla_gate_up[l], gla_gate_bias[l], gla_norm[l],
                          rwkv_mu[l], rwkv_w_up[l], rwkv_w0[l], rwkv_a_up[l], rwkv_a0[l], rwkv_g_up[l],
                          rwkv_k_k[l], rwkv_k_a[l], rwkv_r_k[l], rwkv_ln_w[l], rwkv_ln_b[l],
                          w_branch[l], w_out[l], norm_post[l, 1][None, :])
        h = _ffn(h, norm_pre[l, 2][None, :], norm_post[l, 2][None, :],
                 ffn_in[l, 1].astype(BF16), ffn_out[l, 1].astype(BF16))
    return h[None]
```

```python
import functools
import math

import jax
import jax.numpy as jnp
from jax import lax
from jax.experimental import pallas as pl
from jax.experimental.pallas import tpu as pltpu

F32 = jnp.float32
BF16 = jnp.bfloat16

D_MODEL = 2048
BRANCH = D_MODEL // 2
D_FF = 256 * math.ceil(8 * D_MODEL / 3 / 256)
FFN_SCALE = 0.5
NORM_EPS = 1e-6

SB_HEAD_DIM = 64
GLA_HEADS = 4
GLA_DV = BRANCH // GLA_HEADS
GLA_DK = GLA_DV // 2
GLA_RANK = 16
GLA_TAU = 16.0
GLA_CHUNK = 64
RW_N = 64
RW_RANK = 64
RW_GATE_RANK = 160
RW_LN_EPS = 64e-5

LANES = 128
MXU_DIM = 256
VMEM_LIMIT_CAP = 60000 * 1024

C_GLA_Q = 0
C_GLA_K = 512
C_GLA_V = 1024
C_GLA_R = 2048
C_RW_R = 3072
C_RW_K = 4096
C_RW_V = 5120
C_GATES = 6144
C_ALPHA = 12288
C_WA = 12416
C_GLOW = 12544
N_REST = 12800
N_SB = 3 * BRANCH


def _vmem_limit(nbytes):
    return int(min(max(2 * nbytes, 32 * 1024 * 1024), VMEM_LIMIT_CAP))


def _params(sem, nbytes):
    return pltpu.CompilerParams(dimension_semantics=sem, vmem_limit_bytes=_vmem_limit(nbytes))


def _rms(x, gain):
    ms = jnp.mean(x * x, axis=-1, keepdims=True)
    return x * lax.rsqrt(ms + NORM_EPS) * gain


def _softplus(x):
    return jnp.maximum(x, 0.0) + jnp.log(1.0 + jnp.exp(-jnp.abs(x)))


def _split(x, n):
    parts = []
    for _ in range(n - 1):
        p = x.astype(BF16)
        parts.append(p)
        x = x - p.astype(F32)
    parts.append(x.astype(BF16))
    return parts


_NN = (((1,), (0,)), ((), ()))
_NT = (((1,), (1,)), ((), ()))


def _mm(a_parts, b_parts, dims=_NN, order=3):
    acc = None
    for i, a in enumerate(a_parts):
        for j, b in enumerate(b_parts):
            if i + j < order:
                t = lax.dot_general(a, b, dims, preferred_element_type=F32)
                acc = t if acc is None else acc + t
    return acc


def _ffn_kernel(x_ref, gpre_ref, gpost_ref, wg_ref, wu_ref, wo_ref, o_ref, h_ref, acc_ref):
    j = pl.program_id(1)

    @pl.when(j == 0)
    def _():
        h_ref[...] = _rms(x_ref[...], gpre_ref[...]).astype(BF16)
        acc_ref[...] = jnp.zeros_like(acc_ref)

    h = h_ref[...]
    g = jnp.dot(h, wg_ref[...], preferred_element_type=F32)
    u = jnp.dot(h, wu_ref[...], preferred_element_type=F32)
    a = (g * jax.nn.sigmoid(g) * u).astype(BF16)
    acc_ref[...] += jnp.dot(a, wo_ref[...], preferred_element_type=F32)

    @pl.when(j == pl.num_programs(1) - 1)
    def _():
        o_ref[...] = x_ref[...] + FFN_SCALE * _rms(acc_ref[...], gpost_ref[...])


def _ffn(x, gpre, gpost, w_in, w_out, layer, which, tm=512, tf=512):
    s, d = x.shape
    f = w_out.shape[2]
    tm = min(tm, s)
    nj = f // tf
    nbytes = 2 * (2 * tm * d * 4) + tm * d * 6 + 2 * (3 * d * tf * 2) + 4 * tm * tf * 4
    return pl.pallas_call(
        _ffn_kernel,
        grid=(s // tm, nj),
        in_specs=[
            pl.BlockSpec((tm, d), lambda i, j: (i, 0)),
            pl.BlockSpec((1, d), lambda i, j: (0, 0)),
            pl.BlockSpec((1, d), lambda i, j: (0, 0)),
            pl.BlockSpec((None, None, d, tf), lambda i, j: (layer, which, 0, j)),
            pl.BlockSpec((None, None, d, tf), lambda i, j: (layer, which, 0, j + nj)),
            pl.BlockSpec((None, None, tf, d), lambda i, j: (layer, which, j, 0)),
        ],
        out_specs=pl.BlockSpec((tm, d), lambda i, j: (i, 0)),
        out_shape=jax.ShapeDtypeStruct((s, d), F32),
        scratch_shapes=[pltpu.VMEM((tm, d), BF16), pltpu.VMEM((tm, d), F32)],
        compiler_params=_params(("parallel", "arbitrary"), nbytes),
        name="ffn",
    )(x, gpre, gpost, w_in, w_in, w_out)


def _norm_matmul_kernel(x_ref, g_ref, w_ref, o_ref, h_ref):
    @pl.when(pl.program_id(1) == 0)
    def _():
        h_ref[...] = _rms(x_ref[...], g_ref[...]).astype(BF16)

    o_ref[...] = jnp.dot(h_ref[...], w_ref[...], preferred_element_type=F32).astype(o_ref.dtype)


def _norm_matmul(x, gain, w, layer, col0, n, out_dtype, tm=1024, tn=512):
    s, d = x.shape
    tm = min(tm, s)
    nbytes = 2 * tm * d * 4 + tm * d * 2 + 2 * d * tn * 2 + 2 * tm * tn * 4
    return pl.pallas_call(
        _norm_matmul_kernel,
        grid=(s // tm, n // tn),
        in_specs=[
            pl.BlockSpec((tm, d), lambda i, j: (i, 0)),
            pl.BlockSpec((1, d), lambda i, j: (0, 0)),
            pl.BlockSpec((None, d, tn), lambda i, j: (layer, 0, col0 // tn + j)),
        ],
        out_specs=pl.BlockSpec((tm, tn), lambda i, j: (i, j)),
        out_shape=jax.ShapeDtypeStruct((s, n), out_dtype),
        scratch_shapes=[pltpu.VMEM((tm, d), BF16)],
        compiler_params=_params(("parallel", "arbitrary"), nbytes),
        name="norm_matmul",
    )(x, gain, w)


SB_TILE = 256
SB_SUB = 4
SB_EXP_CLAMP = 40.0
SB_LOG_ZERO = -110.0
SB_NO_KEYS = -1e30


def _sb_kernel(q_ref, k_ref, v_ref, o_ref):
    t, nsub = SB_TILE, SB_SUB
    first = nsub * pl.program_id(1)
    left = lax.broadcasted_iota(jnp.int32, (1, LANES), 1) < SB_HEAD_DIM
    q_heads = []
    for s in range(nsub):
        q = q_ref[s * t:(s + 1) * t, :]
        qz = jnp.zeros_like(q)
        q_heads.append((jnp.where(left, q, qz), jnp.where(left, qz, q)))
    row = lax.broadcasted_iota(jnp.int32, (t, t), 0)
    col = lax.broadcasted_iota(jnp.int32, (t, t), 1)
    neg_tri = jnp.where(row >= col, -1.0, 0.0).astype(BF16)
    neg_tri2 = jnp.concatenate([neg_tri, neg_tri], axis=0)
    causal = col < row
    chains = [(s, hh) for s in range(nsub) for hh in range(2)]

    def step(back, diagonal, accs, cs):
        ks, vs = [], []
        for s in range(nsub):
            kb = first + s - back
            if not diagonal:
                cs = [jnp.where(kb >= 0, c, SB_NO_KEYS) if ch[0] == s else c for c, ch in zip(cs, chains)]
                kb = jnp.maximum(kb, 0)
            start = pl.multiple_of(kb * t, t)
            ks.append(k_ref[pl.ds(start, t), :])
            v = v_ref[pl.ds(start, t), :]
            vz = jnp.zeros_like(v)
            vs.append((jnp.where(left, v, vz), jnp.where(left, vz, v)))
        z = [lax.dot_general(q_heads[s][hh], ks[s], _NT, preferred_element_type=F32) for s, hh in chains]
        sp = [jnp.maximum(jnp.log(1.0 + jnp.exp(jnp.minimum(x, SB_EXP_CLAMP))), x) for x in z]
        if diagonal:
            sp = [jnp.where(causal, x, 0.0) for x in sp]
        cum = [jnp.dot(jnp.concatenate(_split(x, 2), axis=1), neg_tri2, preferred_element_type=F32) for x in sp]
        w = [jnp.exp(zz + cc + c) for zz, cc, c in zip(z, cum, cs)]
        if diagonal:
            w = [jnp.where(causal, x, 0.0) for x in w]
        pv = [jnp.dot(x.astype(BF16), vs[s][hh], preferred_element_type=F32) for x, (s, hh) in zip(w, chains)]
        accs = [accs[s] + pv[2 * s] + pv[2 * s + 1] for s in range(nsub)]
        cs = [c + cc[:, :1] for c, cc in zip(cs, cum)]
        live = jnp.max(functools.reduce(jnp.maximum, cs)) >= SB_LOG_ZERO
        return accs, cs, live.astype(jnp.int32)

    accs = [jnp.zeros((t, LANES), F32) for _ in range(nsub)]
    cs = [jnp.zeros((t, 1), F32) for _ in chains]
    accs, cs, live = step(0, True, accs, cs)

    def more(carry):
        return jnp.logical_and(carry[0] < first + nsub, carry[1] > 0)

    def walk(carry):
        accs, cs, live = step(carry[0], False, list(carry[2]), list(carry[3]))
        return carry[0] + 1, live, tuple(accs), tuple(cs)

    carry = lax.while_loop(more, walk, (jnp.int32(1), live, tuple(accs), tuple(cs)))
    for s in range(nsub):
        o_ref[s * t:(s + 1) * t, :] = carry[2][s].astype(o_ref.dtype)


def _sb_attention(qkv):
    s = qkv.shape[0]
    t = SB_TILE
    tq = SB_SUB * t
    npair = BRANCH // LANES
    nbytes = 2 * 2 * s * LANES * 2 + 16 * SB_SUB * t * t * 4
    return pl.pallas_call(
        _sb_kernel,
        grid=(npair, s // tq),
        in_specs=[
            pl.BlockSpec((tq, LANES), lambda p, i: (i, p)),
            pl.BlockSpec((s, LANES), lambda p, i: (0, npair + p)),
            pl.BlockSpec((s, LANES), lambda p, i: (0, 2 * npair + p)),
        ],
        out_specs=pl.BlockSpec((tq, LANES), lambda p, i: (i, p)),
        out_shape=jax.ShapeDtypeStruct((s, BRANCH), BF16),
        compiler_params=_params(("parallel", "arbitrary"), nbytes),
        name="sb_attention",
    )(qkv, qkv, qkv)


GLA_TILE = 256


def _gla_kernel(q_ref, k_ref, v_ref, r_ref, al_ref, gup_ref, gb_ref, gn_ref, o_ref, st_ref):
    t, c, dk, dv = GLA_TILE, GLA_CHUNK, GLA_DK, GLA_DV

    @pl.when(pl.program_id(0) == 0)
    def _():
        st_ref[...] = jnp.zeros_like(st_ref)

    x = jnp.dot(al_ref[...].astype(BF16), gup_ref[...], preferred_element_type=F32) + gb_ref[...]
    g = -_softplus(-x) * (1.0 / GLA_TAU)
    row = lax.broadcasted_iota(jnp.int32, (t, t), 0)
    col = lax.broadcasted_iota(jnp.int32, (t, t), 1)
    same = (row // c) == (col // c)
    tril = same & (col <= row)
    g_parts = _split(g, 3)
    b_all = _mm([jnp.where(tril, 1.0, 0.0).astype(BF16)], g_parts)
    last_all = _mm([jnp.where(same, 1.0, 0.0).astype(BF16)], g_parts)
    hs = range(GLA_HEADS)
    kcols = [slice(h * dk, (h + 1) * dk) for h in hs]
    vcols = [slice(h * dv, (h + 1) * dv) for h in hs]
    b = [b_all[:, sl] for sl in kcols]
    b_last = [last_all[:, sl] for sl in kcols]
    k = [k_ref[:, sl] for sl in kcols]
    v = [v_ref[:, sl].astype(BF16) for sl in vcols]
    q_dec = [(q_ref[:, sl] * (dk ** -0.5) * jnp.exp(bb)).astype(BF16) for sl, bb in zip(kcols, b)]
    k_dec = [(kk * jnp.exp(-bb)).astype(BF16) for kk, bb in zip(k, b)]
    k_tail = [(kk * jnp.exp(bl - bb)).astype(BF16) for kk, bl, bb in zip(k, b_last, b)]
    chunk_decay = [jnp.exp(bl) for bl in b_last]
    scores = [lax.dot_general(qd, kd, _NT, preferred_element_type=F32) for qd, kd in zip(q_dec, k_dec)]
    scores = [jnp.where(tril, sc, 0.0).astype(BF16) for sc in scores]
    o_intra = [jnp.dot(sc, vv, preferred_element_type=F32) for sc, vv in zip(scores, v)]
    v_t = [vv.astype(F32).T.astype(BF16) for vv in v]
    st = [st_ref[h] for h in hs]
    outs = [[] for _ in hs]
    for n in range(t // c):
        rows = slice(n * c, (n + 1) * c)
        o_inter = [lax.dot_general(qd[rows], s.astype(BF16), _NT, preferred_element_type=F32) for qd, s in zip(q_dec, st)]
        for h in hs:
            outs[h].append(o_intra[h][rows] + o_inter[h])
        kv = [jnp.dot(vt[:, rows], kt[rows], preferred_element_type=F32) for vt, kt in zip(v_t, k_tail)]
        st = [s * cd[n * c:n * c + 1, :] + x for s, cd, x in zip(st, chunk_decay, kv)]
    for h in hs:
        st_ref[h] = st[h]
        o = jnp.concatenate(outs[h], axis=0)
        o = o * lax.rsqrt(jnp.mean(o * o, axis=-1, keepdims=True) + NORM_EPS)
        r = r_ref[:, vcols[h]]
        o_ref[:, vcols[h]] = (o * gn_ref[:, vcols[h]] * (r * jax.nn.sigmoid(r))).astype(o_ref.dtype)


def _gla(y, gate_up, gate_bias, gla_norm):
    s = y.shape[0]
    t = min(GLA_TILE, s)
    kw, vw = GLA_HEADS * GLA_DK, GLA_HEADS * GLA_DV
    nbytes = 2 * t * (2 * kw + 3 * vw + LANES) * 4 + GLA_HEADS * 12 * t * t * 4
    return pl.pallas_call(
        _gla_kernel,
        grid=(s // t,),
        in_specs=[
            pl.BlockSpec((t, kw), lambda i: (i, C_GLA_Q // kw)),
            pl.BlockSpec((t, kw), lambda i: (i, C_GLA_K // kw)),
            pl.BlockSpec((t, vw), lambda i: (i, C_GLA_V // vw)),
            pl.BlockSpec((t, vw), lambda i: (i, C_GLA_R // vw)),
            pl.BlockSpec((t, LANES), lambda i: (i, C_ALPHA // LANES)),
            pl.BlockSpec((LANES, kw), lambda i: (0, 0)),
            pl.BlockSpec((1, kw), lambda i: (0, 0)),
            pl.BlockSpec((1, vw), lambda i: (0, 0)),
        ],
        out_specs=pl.BlockSpec((t, vw), lambda i: (i, 0)),
        out_shape=jax.ShapeDtypeStruct((s, BRANCH), BF16),
        scratch_shapes=[pltpu.VMEM((GLA_HEADS, GLA_DV, GLA_DK), F32)],
        compiler_params=_params(("arbitrary",), nbytes),
        name="gla",
    )(y, y, y, y, y, gate_up, gate_bias, gla_norm)


def _head_ones():
    r = lax.broadcasted_iota(jnp.int32, (MXU_DIM, MXU_DIM), 0) // RW_N
    c = lax.broadcasted_iota(jnp.int32, (MXU_DIM, MXU_DIM), 1) // RW_N
    return jnp.where(r == c, 1.0, 0.0).astype(BF16)


def _head_sum(x, ones_bd, n_split):
    outs = []
    for i in range(x.shape[1] // MXU_DIM):
        outs.append(_mm(_split(x[:, i * MXU_DIM:(i + 1) * MXU_DIM], n_split), [ones_bd], order=n_split))
    return jnp.concatenate(outs, axis=1)


def _rwkv_prep_kernel(yr_ref, yk_ref, yv_ref, ywa_ref, yg_ref, pr_ref, pk_ref, pv_ref, pwa_ref, pg_ref,
                      mur_ref, muk_ref, muv_ref, muwa_ref, mug_ref, wup_ref, aup_ref, gup_ref,
                      w0_ref, a0_ref, kk_ref, ka_ref, rk_ref,
                      r_out, lw_out, k_out, v_out, kk_out, a_out, g_out, bonus_out):
    first = pl.program_id(0) == 0

    def shifted(y_ref, p_ref, mu_ref):
        y = y_ref[...]
        prev_row = jnp.where(first, 0.0, p_ref[7:8, :])
        row = lax.broadcasted_iota(jnp.int32, y.shape, 0)
        y_prev = jnp.where(row == 0, prev_row, pltpu.roll(y, 1, axis=0))
        return y + (y_prev - y) * mu_ref[...]

    r = shifted(yr_ref, pr_ref, mur_ref)
    k = shifted(yk_ref, pk_ref, muk_ref)
    v = shifted(yv_ref, pv_ref, muv_ref)
    wa = shifted(ywa_ref, pwa_ref, muwa_ref)
    gl = shifted(yg_ref, pg_ref, mug_ref)
    w_pre = w0_ref[...] + jnp.dot(jnp.tanh(wa).astype(BF16), wup_ref[...], preferred_element_type=F32)
    lw = -jnp.exp(-_softplus(-w_pre) - 0.5)
    a = jax.nn.sigmoid(a0_ref[...] + jnp.dot(wa.astype(BF16), aup_ref[...], preferred_element_type=F32))
    g = jnp.dot(jax.nn.sigmoid(gl).astype(BF16), gup_ref[...], preferred_element_type=F32)
    ones_bd = _head_ones()
    kk = k * kk_ref[...]
    kk = kk / jnp.maximum(jnp.sqrt(_head_sum(kk * kk, ones_bd, 3)), 1e-12)
    k = k * (1.0 + (a - 1.0) * ka_ref[...])
    bonus = _head_sum(r * k * rk_ref[...], ones_bd, 3) * v
    r_out[...] = r
    lw_out[...] = lw
    k_out[...] = k
    v_out[...] = v
    kk_out[...] = kk
    a_out[...] = a
    g_out[...] = g
    bonus_out[...] = bonus


def _rwkv_prep(y, mu_r, mu_k, mu_v, mu_wa, mu_g, wup, aup, gup, w0, a0, k_k, k_a, r_k, tm=512):
    s = y.shape[0]
    tm = min(tm, s)
    w = BRANCH

    def cur(width, col):
        return pl.BlockSpec((tm, width), lambda i: (i, col // width))

    def prev(width, col):
        return pl.BlockSpec((8, width), lambda i: (jnp.maximum(i * (tm // 8) - 1, 0), col // width))

    def full(shape):
        return pl.BlockSpec(shape, lambda i: (0, 0))

    nbytes = 2 * tm * (3 * w + 384) * 4 + 2 * 8 * tm * w * 4 + 12 * tm * w * 4
    out = jax.ShapeDtypeStruct((s, w), F32)
    return pl.pallas_call(
        _rwkv_prep_kernel,
        grid=(s // tm,),
        in_specs=[cur(w, C_RW_R), cur(w, C_RW_K), cur(w, C_RW_V), cur(LANES, C_WA), cur(2 * LANES, C_GLOW),
                  prev(w, C_RW_R), prev(w, C_RW_K), prev(w, C_RW_V), prev(LANES, C_WA), prev(2 * LANES, C_GLOW),
                  full((1, w)), full((1, w)), full((1, w)), full((1, LANES)), full((1, 2 * LANES)),
                  full((LANES, w)), full((LANES, w)), full((2 * LANES, w)),
                  full((1, w)), full((1, w)), full((1, w)), full((1, w)), full((1, w))],
        out_specs=[pl.BlockSpec((tm, w), lambda i: (i, 0))] * 8,
        out_shape=[out] * 8,
        compiler_params=_params(("parallel",), nbytes),
        name="rwkv_prep",
    )(y, y, y, y, y, y, y, y, y, y, mu_r, mu_k, mu_v, mu_wa, mu_g, wup, aup, gup, w0, a0, k_k, k_a, r_k)


RW_CHUNK = 64
RW_PACK = MXU_DIM // RW_N
RW_NEUMANN = 8
RW_STEP_CHUNKS = 2


def _rwkv_kernel(r_ref, lw_ref, k_ref, v_ref, kk_ref, a_ref, o_ref, s_ref):
    c, g4 = RW_CHUNK, MXU_DIM

    @pl.when(pl.program_id(0) == 0)
    def _():
        s_ref[...] = jnp.zeros_like(s_ref)

    ti = lax.broadcasted_iota(jnp.int32, (c, g4), 0)
    ci = lax.broadcasted_iota(jnp.int32, (c, g4), 1) % c
    strict = ci < ti
    incl = ci <= ti
    lane_head = lax.broadcasted_iota(jnp.int32, (1, g4), 1) // RW_N
    head_masks = [lane_head == h for h in range(RW_PACK)]
    bd_mask = (lax.broadcasted_iota(jnp.int32, (g4, g4), 0) // RW_N) == (lax.broadcasted_iota(jnp.int32, (g4, g4), 1) // RW_N)
    tr = lax.broadcasted_iota(jnp.int32, (RW_STEP_CHUNKS * c, RW_STEP_CHUNKS * c), 0)
    tc = lax.broadcasted_iota(jnp.int32, (RW_STEP_CHUNKS * c, RW_STEP_CHUNKS * c), 1)
    tri = jnp.where((tc <= tr) & (tc // c == tr // c), 1.0, 0.0).astype(BF16)

    def heads(b):
        p = b.astype(BF16)
        return jnp.concatenate([jnp.where(m, p, jnp.zeros_like(p)) for m in head_masks], axis=0)

    groups = range(BRANCH // g4)
    lanes = [slice(grp * g4, (grp + 1) * g4) for grp in groups]
    rows = [slice(n * c, (n + 1) * c) for n in range(RW_STEP_CHUNKS)]
    units = [(rw, sl) for rw in rows for sl in lanes]

    def each(fn, *cols):
        return [fn(*args) for args in zip(*cols)]

    def prod(a, b, dims=_NN):
        m = a.shape[0]
        both = lax.dot_general(jnp.concatenate(_split(a, 2), axis=0), b, dims, preferred_element_type=F32)
        return both[:m] + both[m:]

    lw_all = lw_ref[...]
    cum_all = _mm([tri], _split(lw_all, 3))
    lw = [lw_all[rw, sl] for rw, sl in units]
    cum = [cum_all[rw, sl] for rw, sl in units]
    cum_end = [x[c - 1:c, :] for x in cum]
    kk = [kk_ref[rw, sl] for rw, sl in units]
    kka = [x * a_ref[rw, sl] for x, (rw, sl) in zip(kk, units)]
    k = [k_ref[rw, sl] for rw, sl in units]
    v = [v_ref[rw, sl] for rw, sl in units]
    g_inv = each(lambda x: jnp.exp(-x), cum)
    tail = each(lambda e, x: jnp.exp(e - x), cum_end, cum)
    alp = each(lambda x, cm, l: x * jnp.exp(cm - l), kk, cum, lw)
    rho = each(lambda un, cm: r_ref[un[0], un[1]] * jnp.exp(cm), units, cum)
    ar = each(lambda x, y: jnp.concatenate([x, y], axis=0), alp, rho)
    pb = each(lambda x, y, gi: prod(x, heads(y * gi), _NT), ar, kka, g_inv)
    pk = each(lambda x, y, gi: prod(x, heads(y * gi), _NT), ar, k, g_inv)
    l_b = each(lambda p: jnp.where(strict, p[:c], 0.0), pb)
    m_b = each(lambda p: jnp.where(incl, p[c:], 0.0), pb)
    lm_k = each(lambda p: jnp.concatenate([jnp.where(strict, p[:c], 0.0), jnp.where(incl, p[c:], 0.0)], axis=0), pk)

    nb = RW_NEUMANN
    eye = jnp.where(ci == ti, 1.0, 0.0)
    l1 = each(lambda ll: jnp.where((ci // nb) == (ti // nb), ll, 0.0), l_b)
    l2 = each(lambda ll: prod(ll, heads(ll)), l1)
    l34 = each(lambda ll, sq: prod(jnp.concatenate([ll, sq], axis=0), heads(sq)), l1, l2)
    a1 = each(lambda ll, sq, p: eye - ll + sq - p[:c], l1, l2, l34)
    x = each(lambda aa, p: aa + prod(aa, heads(p[c:])), a1, l34)
    m = nb
    while m < c:
        lower = ((ti // m) % 2 == 1) & ((ci // m) == (ti // m) - 1)
        y = each(lambda xx, ll: prod(xx, heads(jnp.where(lower, ll, 0.0))), x, l_b)
        x = each(lambda xx, yy: xx - prod(yy, heads(xx)), x, y)
        m *= 2

    lmv = each(lambda lm, vv: prod(lm, heads(vv)), lm_k, v)
    kb = each(lambda kx, ka, t: jnp.concatenate([kx * t, ka * t], axis=0).astype(BF16), k, kka, tail)
    g_end = each(jnp.exp, cum_end)

    st = [s_ref[grp] for grp in groups]
    ng = len(lanes)
    for n, rw in enumerate(rows):
        mine = slice(n * ng, (n + 1) * ng)
        ars = each(lambda a, s: prod(a, s.astype(BF16), _NT), ar[mine], st)
        u = each(lambda xx, a, p: prod(xx, heads(a[:c] + p[:c])), x[mine], ars, lmv[mine])
        o = each(lambda a, p, mb, uu: a[c:] + p[c:] - prod(mb, heads(uu)), ars, lmv[mine], m_b[mine], u)
        vu_t = each(lambda vv, uu: jnp.concatenate([vv, -uu], axis=0).T, v[mine], u)
        st = each(lambda s, e, a, b: jnp.where(bd_mask, s * e + prod(a, b), 0.0), st, g_end[mine], vu_t, kb[mine])
        for grp in groups:
            o_ref[rw, lanes[grp]] = o[grp]
    for grp in groups:
        s_ref[grp] = st[grp]


def _rwkv_recurrence(r, lw, k, v, kk, a):
    s, w = r.shape
    c = RW_STEP_CHUNKS * RW_CHUNK
    spec = pl.BlockSpec((c, w), lambda i: (i, 0))
    nbytes = 2 * 7 * c * w * 4 + (w // MXU_DIM) * MXU_DIM * MXU_DIM * 4 + RW_STEP_CHUNKS * 64 * MXU_DIM * MXU_DIM * 4
    return pl.pallas_call(
        _rwkv_kernel,
        grid=(s // c,),
        in_specs=[spec] * 6,
        out_specs=spec,
        out_shape=jax.ShapeDtypeStruct((s, w), F32),
        scratch_shapes=[pltpu.VMEM((w // MXU_DIM, MXU_DIM, MXU_DIM), F32)],
        compiler_params=_params(("arbitrary",), nbytes),
        name="rwkv_recurrence",
    )(r, lw, k, v, kk, a)


def _rwkv_post_kernel(wkv_ref, g_ref, bonus_ref, lnw_ref, lnb_ref, o_ref):
    ones_bd = _head_ones()
    x = wkv_ref[...]
    d = x - _head_sum(x, ones_bd, 3) * (1.0 / RW_N)
    var = _head_sum(d * d, ones_bd, 3) * (1.0 / RW_N)
    y = d * lax.rsqrt(var + RW_LN_EPS) * lnw_ref[...] + lnb_ref[...]
    o_ref[...] = ((y + bonus_ref[...]) * g_ref[...]).astype(o_ref.dtype)


def _rwkv_post(wkv, g, bonus, ln_w, ln_b, tm=512):
    s, w = wkv.shape
    tm = min(tm, s)
    blk = pl.BlockSpec((tm, w), lambda i: (i, 0))
    vec = pl.BlockSpec((1, w), lambda i: (0, 0))
    nbytes = 2 * 4 * tm * w * 4 + 8 * tm * w * 4
    return pl.pallas_call(
        _rwkv_post_kernel,
        grid=(s // tm,),
        in_specs=[blk, blk, blk, vec, vec],
        out_specs=blk,
        out_shape=jax.ShapeDtypeStruct((s, w), BF16),
        compiler_params=_params(("parallel",), nbytes),
        name="rwkv_post",
    )(wkv, g, bonus, ln_w, ln_b)


def _merge_kernel(oa_ref, ob_ref, oc_ref, wb_ref, ga_ref, gb_ref, gc_ref, o_ref):
    acc = None
    for o_g, gate, idx in ((oa_ref, ga_ref, 0), (ob_ref, gb_ref, 1), (oc_ref, gc_ref, 2)):
        u = jnp.dot(o_g[...], wb_ref[idx], preferred_element_type=F32)
        t = jax.nn.sigmoid(gate[...]) * u
        acc = t if acc is None else acc + t
    o_ref[...] = acc.astype(o_ref.dtype)


def _merge(oa, ob, oc, w_branch, layer, y, tm=1024, tn=512):
    s, w = oa.shape
    d = w_branch.shape[3]
    tm = min(tm, s)
    branch = pl.BlockSpec((tm, w), lambda i, j: (i, 0))

    def gate(gi):
        return pl.BlockSpec((tm, tn), lambda i, j: (i, (C_GATES + gi * d) // tn + j))

    nbytes = 2 * 3 * tm * w * 2 + 2 * 3 * w * tn * 2 + 2 * 4 * tm * tn * 4 + 4 * tm * tn * 4
    return pl.pallas_call(
        _merge_kernel,
        grid=(s // tm, d // tn),
        in_specs=[branch, branch, branch, pl.BlockSpec((None, 3, w, tn), lambda i, j: (layer, 0, 0, j)),
                  gate(0), gate(1), gate(2)],
        out_specs=pl.BlockSpec((tm, tn), lambda i, j: (i, j)),
        out_shape=jax.ShapeDtypeStruct((s, d), BF16),
        compiler_params=_params(("parallel", "arbitrary"), nbytes),
        name="merge",
    )(oa, ob, oc, w_branch, y, y, y)


def _out_proj_kernel(m_ref, w_ref, x_ref, g_ref, o_ref):
    mix = jnp.dot(m_ref[...], w_ref[...], preferred_element_type=F32)
    o_ref[...] = x_ref[...] + _rms(mix, g_ref[...])


def _out_proj(merged, w_out, layer, x, gain, tm=512):
    s, d = x.shape
    tm = min(tm, s)
    nbytes = 2 * tm * d * 2 + 2 * d * d * 2 + 4 * tm * d * 4 + 2 * tm * d * 4
    return pl.pallas_call(
        _out_proj_kernel,
        grid=(s // tm,),
        in_specs=[pl.BlockSpec((tm, d), lambda i: (i, 0)), pl.BlockSpec((None, d, d), lambda i: (layer, 0, 0)),
                  pl.BlockSpec((tm, d), lambda i: (i, 0)), pl.BlockSpec((1, d), lambda i: (0, 0))],
        out_specs=pl.BlockSpec((tm, d), lambda i: (i, 0)),
        out_shape=jax.ShapeDtypeStruct((s, d), F32),
        compiler_params=_params(("parallel",), nbytes),
        name="out_proj",
    )(merged, w_out, x, gain)


def _pad_cols(w, width):
    return jnp.pad(w, [(0, 0)] * (w.ndim - 1) + [(0, width - w.shape[-1])])


def _pad_rows(w, height, offset=0):
    return jnp.pad(w, ((offset, height - offset - w.shape[0]), (0, 0)))


def _pack_w_in(w):
    b = BRANCH
    gla0 = 3 * b
    alpha0 = gla0 + 2 * GLA_HEADS * GLA_DK + 2 * b
    rw0 = alpha0 + GLA_RANK
    wa0 = rw0 + 3 * b
    gl0 = wa0 + 2 * RW_RANK
    gates0 = gl0 + RW_GATE_RANK
    return jnp.concatenate([
        w[..., :b] * (SB_HEAD_DIM ** -0.5), w[..., b:3 * b],
        w[..., gla0:alpha0], w[..., rw0:wa0], w[..., gates0:],
        _pad_cols(w[..., alpha0:rw0], LANES), w[..., wa0:gl0], _pad_cols(w[..., gl0:gates0], 2 * LANES)],
        axis=-1).astype(BF16)


def _token_mixing(x, gain, w_packed, layer, gla_gate_up, gla_gate_bias, gla_norm, rwkv_mu, rwkv_w_up, rwkv_w0,
                  rwkv_a_up, rwkv_a0, rwkv_g_up, rwkv_k_k, rwkv_k_a, rwkv_r_k, rwkv_ln_w, rwkv_ln_b,
                  w_branch, w_out, gain_post):
    b = BRANCH
    qkv = _norm_matmul(x, gain, w_packed, layer, 0, N_SB, BF16)
    y = _norm_matmul(x, gain, w_packed, layer, N_SB, N_REST, F32)

    o_a = _sb_attention(qkv)

    o_b = _gla(y, _pad_rows(gla_gate_up, LANES).astype(BF16), gla_gate_bias[None, :], gla_norm[None, :])

    mu = rwkv_mu
    row = lambda t: t[None, :]
    prep = _rwkv_prep(
        y, row(mu[:b]), row(mu[b:2 * b]), row(mu[2 * b:3 * b]), row(mu[3 * b:3 * b + 2 * RW_RANK]),
        _pad_cols(row(mu[3 * b + 2 * RW_RANK:]), 2 * LANES),
        _pad_rows(rwkv_w_up, LANES).astype(BF16), _pad_rows(rwkv_a_up, LANES, RW_RANK).astype(BF16),
        _pad_rows(rwkv_g_up, 2 * LANES).astype(BF16),
        row(rwkv_w0), row(rwkv_a0), row(rwkv_k_k), row(rwkv_k_a), row(rwkv_r_k.reshape(-1)))
    r, lw, k, v, kk, a, g, bonus = prep
    wkv = _rwkv_recurrence(r, lw, k, v, kk, a)
    o_c = _rwkv_post(wkv, g, bonus, row(rwkv_ln_w), row(rwkv_ln_b))

    merged = _merge(o_a, o_b, o_c, w_branch, layer, y)
    return _out_proj(merged, w_out, layer, x, gain_post)


def kernel(x, norm_pre, norm_post, ffn_in, ffn_out, w_in, gla_gate_up, gla_gate_bias, gla_norm, rwkv_mu,
           rwkv_w_up, rwkv_w0, rwkv_a_up, rwkv_a0, rwkv_g_up, rwkv_k_k, rwkv_k_a, rwkv_r_k, rwkv_ln_w,
           rwkv_ln_b, w_branch, w_out):
    batch, seq, d = x.shape
    assert batch == 1 and d == D_MODEL
    h = x[0]
    ffn_in, ffn_out = ffn_in.astype(BF16), ffn_out.astype(BF16)
    w_packed = _pack_w_in(w_in)
    w_branch, w_out = w_branch.astype(BF16), w_out.astype(BF16)
    for l in range(norm_pre.shape[0]):
        h = _ffn(h, norm_pre[l, 0][None, :], norm_post[l, 0][None, :], ffn_in, ffn_out, l, 0)
        h = _token_mixing(h, norm_pre[l, 1][None, :], w_packed, l, gla_gate_up[l], gla_gate_bias[l], gla_norm[l],
                          rwkv_mu[l], rwkv_w_up[l], rwkv_w0[l], rwkv_a_up[l], rwkv_a0[l], rwkv_g_up[l],
                          rwkv_k_k[l], rwkv_k_a[l], rwkv_r_k[l], rwkv_ln_w[l], rwkv_ln_b[l],
                          w_branch, w_out, norm_post[l, 1][None, :])
        h = _ffn(h, norm_pre[l, 2][None, :], norm_post[l, 2][None, :], ffn_in, ffn_out, l, 1)
    return h[None]
```

```python
import functools
import math

import jax
import jax.numpy as jnp
from jax import lax
from jax.experimental import pallas as pl
from jax.experimental.pallas import tpu as pltpu

F32 = jnp.float32
BF16 = jnp.bfloat16

D_MODEL = 2048
BRANCH = D_MODEL // 2
D_FF = 256 * math.ceil(8 * D_MODEL / 3 / 256)
FFN_SCALE = 0.5
NORM_EPS = 1e-6

SB_HEAD_DIM = 64
GLA_HEADS = 4
GLA_DV = BRANCH // GLA_HEADS
GLA_DK = GLA_DV // 2
GLA_RANK = 16
GLA_TAU = 16.0
GLA_CHUNK = 64
RW_N = 64
RW_RANK = 64
RW_GATE_RANK = 160
RW_LN_EPS = 64e-5

LANES = 128
MXU_DIM = 256
VMEM_LIMIT_CAP = 60000 * 1024

C_GLA_Q = 0
C_GLA_K = 512
C_GLA_V = 1024
C_GLA_R = 2048
C_RW_R = 3072
C_RW_K = 4096
C_RW_V = 5120
C_GATES = 6144
C_ALPHA = 12288
C_WA = 12416
C_GLOW = 12544
N_REST = 12800
N_SB = 3 * BRANCH


def _vmem_limit(nbytes):
    return int(min(max(2 * nbytes, 32 * 1024 * 1024), VMEM_LIMIT_CAP))


def _params(sem, nbytes):
    return pltpu.CompilerParams(dimension_semantics=sem, vmem_limit_bytes=_vmem_limit(nbytes))


def _rms(x, gain):
    ms = jnp.mean(x * x, axis=-1, keepdims=True)
    return x * lax.rsqrt(ms + NORM_EPS) * gain


def _softplus(x):
    return jnp.maximum(x, 0.0) + jnp.log(1.0 + jnp.exp(-jnp.abs(x)))


def _split(x, n):
    parts = []
    for _ in range(n - 1):
        p = x.astype(BF16)
        parts.append(p)
        x = x - p.astype(F32)
    parts.append(x.astype(BF16))
    return parts


_NN = (((1,), (0,)), ((), ()))
_NT = (((1,), (1,)), ((), ()))


def _mm(a_parts, b_parts, dims=_NN, order=3):
    acc = None
    for i, a in enumerate(a_parts):
        for j, b in enumerate(b_parts):
            if i + j < order:
                t = lax.dot_general(a, b, dims, preferred_element_type=F32)
                acc = t if acc is None else acc + t
    return acc


def _ffn_kernel(x_ref, gpre_ref, gpost_ref, wg_ref, wu_ref, wo_ref, o_ref, h_ref, acc_ref):
    j = pl.program_id(1)

    @pl.when(j == 0)
    def _():
        h_ref[...] = _rms(x_ref[...], gpre_ref[...]).astype(BF16)
        acc_ref[...] = jnp.zeros_like(acc_ref)

    h = h_ref[...]
    g = jnp.dot(h, wg_ref[...].astype(BF16), preferred_element_type=F32)
    u = jnp.dot(h, wu_ref[...].astype(BF16), preferred_element_type=F32)
    a = (g * jax.nn.sigmoid(g) * u).astype(BF16)
    acc_ref[...] += jnp.dot(a, wo_ref[...].astype(BF16), preferred_element_type=F32)

    @pl.when(j == pl.num_programs(1) - 1)
    def _():
        o_ref[...] = x_ref[...] + FFN_SCALE * _rms(acc_ref[...], gpost_ref[...])


def _ffn(x, gpre, gpost, w_in, w_out, layer, which, tm=1024, tf=256):
    s, d = x.shape
    f = w_out.shape[2]
    tm = min(tm, s)
    nj = f // tf
    nbytes = (2 * tm * d * 4) + tm * d * 6 + 2 * (3 * d * tf * 4) + 3 * d * tf * 2 + 4 * tm * tf * 4
    return pl.pallas_call(
        _ffn_kernel,
        grid=(s // tm, nj),
        in_specs=[
            pl.BlockSpec((tm, d), lambda i, j: (i, 0)),
            pl.BlockSpec((1, d), lambda i, j: (0, 0)),
            pl.BlockSpec((1, d), lambda i, j: (0, 0)),
            pl.BlockSpec((None, None, d, tf), lambda i, j: (layer, which, 0, j)),
            pl.BlockSpec((None, None, d, tf), lambda i, j: (layer, which, 0, j + nj)),
            pl.BlockSpec((None, None, tf, d), lambda i, j: (layer, which, j, 0)),
        ],
        out_specs=pl.BlockSpec((tm, d), lambda i, j: (i, 0), pipeline_mode=pl.Buffered(1)),
        out_shape=jax.ShapeDtypeStruct((s, d), F32),
        scratch_shapes=[pltpu.VMEM((tm, d), BF16), pltpu.VMEM((tm, d), F32)],
        compiler_params=_params(("parallel", "arbitrary"), nbytes),
        name="ffn",
    )(x, gpre, gpost, w_in, w_in, w_out)


def _norm_matmul_kernel(x_ref, g_ref, w_ref, o_ref, h_ref):
    @pl.when(pl.program_id(1) == 0)
    def _():
        h_ref[...] = _rms(x_ref[...], g_ref[...]).astype(BF16)

    o_ref[...] = lax.dot_general(h_ref[...], w_ref[...], _NT, preferred_element_type=F32).astype(o_ref.dtype)


def _norm_matmul(x, gain, w, layer, col0, n, out_dtype, tm=1024, tn=512):
    s, d = x.shape
    tm = min(tm, s)
    nbytes = 2 * tm * d * 4 + tm * d * 2 + 2 * d * tn * 2 + 2 * tm * tn * 4
    return pl.pallas_call(
        _norm_matmul_kernel,
        grid=(s // tm, n // tn),
        in_specs=[
            pl.BlockSpec((tm, d), lambda i, j: (i, 0)),
            pl.BlockSpec((1, d), lambda i, j: (0, 0)),
            pl.BlockSpec((None, tn, d), lambda i, j: (layer, col0 // tn + j, 0)),
        ],
        out_specs=pl.BlockSpec((tm, tn), lambda i, j: (i, j)),
        out_shape=jax.ShapeDtypeStruct((s, n), out_dtype),
        scratch_shapes=[pltpu.VMEM((tm, d), BF16)],
        compiler_params=_params(("parallel", "arbitrary"), nbytes),
        name="norm_matmul",
    )(x, gain, w)


SB_TILE = 256
SB_SUB = 4
SB_EXP_CLAMP = 40.0
SB_LOG_ZERO = -110.0
SB_NO_KEYS = -1e30


def _sb_kernel(q_ref, k_ref, v_ref, o_ref):
    t, nsub = SB_TILE, SB_SUB
    first = nsub * pl.program_id(1)
    left = lax.broadcasted_iota(jnp.int32, (1, LANES), 1) < SB_HEAD_DIM
    q_heads = []
    for s in range(nsub):
        q = q_ref[s * t:(s + 1) * t, :]
        qz = jnp.zeros_like(q)
        q_heads.append((jnp.where(left, q, qz), jnp.where(left, qz, q)))
    row = lax.broadcasted_iota(jnp.int32, (t, t), 0)
    col = lax.broadcasted_iota(jnp.int32, (t, t), 1)
    neg_tri = jnp.where(row >= col, -1.0, 0.0).astype(BF16)
    neg_tri2 = jnp.concatenate([neg_tri, neg_tri], axis=0)
    causal = col < row
    chains = [(s, hh) for s in range(nsub) for hh in range(2)]

    def step(back, diagonal, accs, cs):
        ks, vs = [], []
        for s in range(nsub):
            kb = first + s - back
            if not diagonal:
                cs = [jnp.where(kb >= 0, c, SB_NO_KEYS) if ch[0] == s else c for c, ch in zip(cs, chains)]
                kb = jnp.maximum(kb, 0)
            start = pl.multiple_of(kb * t, t)
            ks.append(k_ref[pl.ds(start, t), :])
            v = v_ref[pl.ds(start, t), :]
            vz = jnp.zeros_like(v)
            vs.append((jnp.where(left, v, vz), jnp.where(left, vz, v)))
        z = [lax.dot_general(q_heads[s][hh], ks[s], _NT, preferred_element_type=F32) for s, hh in chains]
        sp = [jnp.maximum(jnp.log(1.0 + jnp.exp(jnp.minimum(x, SB_EXP_CLAMP))), x) for x in z]
        if diagonal:
            sp = [jnp.where(causal, x, 0.0) for x in sp]
        cum = [jnp.dot(jnp.concatenate(_split(x, 2), axis=1), neg_tri2, preferred_element_type=F32) for x in sp]
        w = [jnp.exp(zz + cc + c) for zz, cc, c in zip(z, cum, cs)]
        if diagonal:
            w = [jnp.where(causal, x, 0.0) for x in w]
        pv = [jnp.dot(x.astype(BF16), vs[s][hh], preferred_element_type=F32) for x, (s, hh) in zip(w, chains)]
        accs = [accs[s] + pv[2 * s] + pv[2 * s + 1] for s in range(nsub)]
        cs = [c + cc[:, :1] for c, cc in zip(cs, cum)]
        live = jnp.max(functools.reduce(jnp.maximum, cs)) >= SB_LOG_ZERO
        return accs, cs, live.astype(jnp.int32)

    accs = [jnp.zeros((t, LANES), F32) for _ in range(nsub)]
    cs = [jnp.zeros((t, 1), F32) for _ in chains]
    accs, cs, live = step(0, True, accs, cs)

    def more(carry):
        return jnp.logical_and(carry[0] < first + nsub, carry[1] > 0)

    def walk(carry):
        accs, cs, live = step(carry[0], False, list(carry[2]), list(carry[3]))
        return carry[0] + 1, live, tuple(accs), tuple(cs)

    carry = lax.while_loop(more, walk, (jnp.int32(1), live, tuple(accs), tuple(cs)))
    for s in range(nsub):
        o_ref[s * t:(s + 1) * t, :] = carry[2][s].astype(o_ref.dtype)


def _sb_attention(qkv):
    s = qkv.shape[0]
    t = SB_TILE
    tq = SB_SUB * t
    npair = BRANCH // LANES
    nbytes = 2 * 2 * s * LANES * 2 + 16 * SB_SUB * t * t * 4
    return pl.pallas_call(
        _sb_kernel,
        grid=(npair, s // tq),
        in_specs=[
            pl.BlockSpec((tq, LANES), lambda p, i: (i, p)),
            pl.BlockSpec((s, LANES), lambda p, i: (0, npair + p)),
            pl.BlockSpec((s, LANES), lambda p, i: (0, 2 * npair + p)),
        ],
        out_specs=pl.BlockSpec((tq, LANES), lambda p, i: (i, p)),
        out_shape=jax.ShapeDtypeStruct((s, BRANCH), BF16),
        compiler_params=_params(("parallel", "arbitrary"), nbytes),
        name="sb_attention",
    )(qkv, qkv, qkv)


GLA_TILE = 256


def _gla_kernel(q_ref, k_ref, v_ref, r_ref, al_ref, gup_ref, gb_ref, gn_ref, o_ref, st_ref):
    t, c, dk, dv = GLA_TILE, GLA_CHUNK, GLA_DK, GLA_DV

    @pl.when(pl.program_id(0) == 0)
    def _():
        st_ref[...] = jnp.zeros_like(st_ref)

    x = jnp.dot(al_ref[...].astype(BF16), gup_ref[...], preferred_element_type=F32) + gb_ref[...]
    g = -_softplus(-x) * (1.0 / GLA_TAU)
    row = lax.broadcasted_iota(jnp.int32, (t, t), 0)
    col = lax.broadcasted_iota(jnp.int32, (t, t), 1)
    same = (row // c) == (col // c)
    tril = same & (col <= row)
    g_parts = _split(g, 3)
    b_all = _mm([jnp.where(tril, 1.0, 0.0).astype(BF16)], g_parts)
    last_all = _mm([jnp.where(same, 1.0, 0.0).astype(BF16)], g_parts)
    hs = range(GLA_HEADS)
    kcols = [slice(h * dk, (h + 1) * dk) for h in hs]
    vcols = [slice(h * dv, (h + 1) * dv) for h in hs]
    b = [b_all[:, sl] for sl in kcols]
    b_last = [last_all[:, sl] for sl in kcols]
    k = [k_ref[:, sl] for sl in kcols]
    v = [v_ref[:, sl].astype(BF16) for sl in vcols]
    q_dec = [(q_ref[:, sl] * (dk ** -0.5) * jnp.exp(bb)).astype(BF16) for sl, bb in zip(kcols, b)]
    k_dec = [(kk * jnp.exp(-bb)).astype(BF16) for kk, bb in zip(k, b)]
    k_tail = [(kk * jnp.exp(bl - bb)).astype(BF16) for kk, bl, bb in zip(k, b_last, b)]
    chunk_decay = [jnp.exp(bl) for bl in b_last]
    scores = [lax.dot_general(qd, kd, _NT, preferred_element_type=F32) for qd, kd in zip(q_dec, k_dec)]
    scores = [jnp.where(tril, sc, 0.0).astype(BF16) for sc in scores]
    o_intra = [jnp.dot(sc, vv, preferred_element_type=F32) for sc, vv in zip(scores, v)]
    v_t = [vv.astype(F32).T.astype(BF16) for vv in v]
    st = [st_ref[h] for h in hs]
    outs = [[] for _ in hs]
    for n in range(t // c):
        rows = slice(n * c, (n + 1) * c)
        o_inter = [lax.dot_general(qd[rows], s.astype(BF16), _NT, preferred_element_type=F32) for qd, s in zip(q_dec, st)]
        for h in hs:
            outs[h].append(o_intra[h][rows] + o_inter[h])
        kv = [jnp.dot(vt[:, rows], kt[rows], preferred_element_type=F32) for vt, kt in zip(v_t, k_tail)]
        st = [s * cd[n * c:n * c + 1, :] + x for s, cd, x in zip(st, chunk_decay, kv)]
    for h in hs:
        st_ref[h] = st[h]
        o = jnp.concatenate(outs[h], axis=0)
        o = o * lax.rsqrt(jnp.mean(o * o, axis=-1, keepdims=True) + NORM_EPS)
        r = r_ref[:, vcols[h]]
        o_ref[:, vcols[h]] = (o * gn_ref[:, vcols[h]] * (r * jax.nn.sigmoid(r))).astype(o_ref.dtype)


def _gla(y, gate_up, gate_bias, gla_norm):
    s = y.shape[0]
    t = min(GLA_TILE, s)
    kw, vw = GLA_HEADS * GLA_DK, GLA_HEADS * GLA_DV
    nbytes = 2 * t * (2 * kw + 3 * vw + LANES) * 4 + GLA_HEADS * 12 * t * t * 4
    return pl.pallas_call(
        _gla_kernel,
        grid=(s // t,),
        in_specs=[
            pl.BlockSpec((t, kw), lambda i: (i, C_GLA_Q // kw)),
            pl.BlockSpec((t, kw), lambda i: (i, C_GLA_K // kw)),
            pl.BlockSpec((t, vw), lambda i: (i, C_GLA_V // vw)),
            pl.BlockSpec((t, vw), lambda i: (i, C_GLA_R // vw)),
            pl.BlockSpec((t, LANES), lambda i: (i, C_ALPHA // LANES)),
            pl.BlockSpec((LANES, kw), lambda i: (0, 0)),
            pl.BlockSpec((1, kw), lambda i: (0, 0)),
            pl.BlockSpec((1, vw), lambda i: (0, 0)),
        ],
        out_specs=pl.BlockSpec((t, vw), lambda i: (i, 0)),
        out_shape=jax.ShapeDtypeStruct((s, BRANCH), BF16),
        scratch_shapes=[pltpu.VMEM((GLA_HEADS, GLA_DV, GLA_DK), F32)],
        compiler_params=_params(("arbitrary",), nbytes),
        name="gla",
    )(y, y, y, y, y, gate_up, gate_bias, gla_norm)


def _head_ones():
    r = lax.broadcasted_iota(jnp.int32, (MXU_DIM, MXU_DIM), 0) // RW_N
    c = lax.broadcasted_iota(jnp.int32, (MXU_DIM, MXU_DIM), 1) // RW_N
    return jnp.where(r == c, 1.0, 0.0).astype(BF16)


def _head_sum(x, ones_bd, n_split):
    outs = []
    for i in range(x.shape[1] // MXU_DIM):
        outs.append(_mm(_split(x[:, i * MXU_DIM:(i + 1) * MXU_DIM], n_split), [ones_bd], order=n_split))
    return jnp.concatenate(outs, axis=1)


def _rwkv_prep_kernel(yr_ref, yk_ref, yv_ref, ywa_ref, yg_ref, pr_ref, pk_ref, pv_ref, pwa_ref, pg_ref,
                      mur_ref, muk_ref, muv_ref, muwa_ref, mug_ref, wup_ref, aup_ref, gup_ref,
                      w0_ref, a0_ref, kk_ref, ka_ref, rk_ref,
                      r_out, lw_out, k_out, v_out, kk_out, a_out, g_out, bonus_out):
    first = pl.program_id(0) == 0

    def shifted(y_ref, p_ref, mu_ref):
        y = y_ref[...]
        prev_row = jnp.where(first, 0.0, p_ref[7:8, :])
        row = lax.broadcasted_iota(jnp.int32, y.shape, 0)
        y_prev = jnp.where(row == 0, prev_row, pltpu.roll(y, 1, axis=0))
        return y + (y_prev - y) * mu_ref[...]

    r = shifted(yr_ref, pr_ref, mur_ref)
    k = shifted(yk_ref, pk_ref, muk_ref)
    v = shifted(yv_ref, pv_ref, muv_ref)
    wa = shifted(ywa_ref, pwa_ref, muwa_ref)
    gl = shifted(yg_ref, pg_ref, mug_ref)
    w_pre = w0_ref[...] + jnp.dot(jnp.tanh(wa).astype(BF16), wup_ref[...], preferred_element_type=F32)
    lw = -jnp.exp(-_softplus(-w_pre) - 0.5)
    a = jax.nn.sigmoid(a0_ref[...] + jnp.dot(wa.astype(BF16), aup_ref[...], preferred_element_type=F32))
    g = jnp.dot(jax.nn.sigmoid(gl).astype(BF16), gup_ref[...], preferred_element_type=F32)
    ones_bd = _head_ones()
    kk = k * kk_ref[...]
    kk = kk / jnp.maximum(jnp.sqrt(_head_sum(kk * kk, ones_bd, 3)), 1e-12)
    k = k * (1.0 + (a - 1.0) * ka_ref[...])
    bonus = _head_sum(r * k * rk_ref[...], ones_bd, 3) * v
    r_out[...] = r
    lw_out[...] = lw
    k_out[...] = k
    v_out[...] = v
    kk_out[...] = kk
    a_out[...] = a
    g_out[...] = g
    bonus_out[...] = bonus


def _rwkv_prep(y, mu_r, mu_k, mu_v, mu_wa, mu_g, wup, aup, gup, w0, a0, k_k, k_a, r_k, tm=512):
    s = y.shape[0]
    tm = min(tm, s)
    w = BRANCH

    def cur(width, col):
        return pl.BlockSpec((tm, width), lambda i: (i, col // width))

    def prev(width, col):
        return pl.BlockSpec((8, width), lambda i: (jnp.maximum(i * (tm // 8) - 1, 0), col // width))

    def full(shape):
        return pl.BlockSpec(shape, lambda i: (0, 0))

    nbytes = 2 * tm * (3 * w + 384) * 4 + 2 * 8 * tm * w * 4 + 12 * tm * w * 4
    out = jax.ShapeDtypeStruct((s, w), F32)
    return pl.pallas_call(
        _rwkv_prep_kernel,
        grid=(s // tm,),
        in_specs=[cur(w, C_RW_R), cur(w, C_RW_K), cur(w, C_RW_V), cur(LANES, C_WA), cur(2 * LANES, C_GLOW),
                  prev(w, C_RW_R), prev(w, C_RW_K), prev(w, C_RW_V), prev(LANES, C_WA), prev(2 * LANES, C_GLOW),
                  full((1, w)), full((1, w)), full((1, w)), full((1, LANES)), full((1, 2 * LANES)),
                  full((LANES, w)), full((LANES, w)), full((2 * LANES, w)),
                  full((1, w)), full((1, w)), full((1, w)), full((1, w)), full((1, w))],
        out_specs=[pl.BlockSpec((tm, w), lambda i: (i, 0))] * 8,
        out_shape=[out] * 8,
        compiler_params=_params(("parallel",), nbytes),
        name="rwkv_prep",
    )(y, y, y, y, y, y, y, y, y, y, mu_r, mu_k, mu_v, mu_wa, mu_g, wup, aup, gup, w0, a0, k_k, k_a, r_k)


RW_CHUNK = 64
RW_PACK = MXU_DIM // RW_N
RW_NEUMANN = 8
RW_STEP_CHUNKS = 2


def _rwkv_kernel(r_ref, lw_ref, k_ref, v_ref, kk_ref, a_ref, o_ref, s_ref):
    c, g4 = RW_CHUNK, MXU_DIM

    @pl.when(pl.program_id(0) == 0)
    def _():
        s_ref[...] = jnp.zeros_like(s_ref)

    ti = lax.broadcasted_iota(jnp.int32, (c, g4), 0)
    ci = lax.broadcasted_iota(jnp.int32, (c, g4), 1) % c
    strict = ci < ti
    incl = ci <= ti
    lane_head = lax.broadcasted_iota(jnp.int32, (1, g4), 1) // RW_N
    head_masks = [lane_head == h for h in range(RW_PACK)]
    bd_mask = (lax.broadcasted_iota(jnp.int32, (g4, g4), 0) // RW_N) == (lax.broadcasted_iota(jnp.int32, (g4, g4), 1) // RW_N)
    tr = lax.broadcasted_iota(jnp.int32, (RW_STEP_CHUNKS * c, RW_STEP_CHUNKS * c), 0)
    tc = lax.broadcasted_iota(jnp.int32, (RW_STEP_CHUNKS * c, RW_STEP_CHUNKS * c), 1)
    tri = jnp.where((tc <= tr) & (tc // c == tr // c), 1.0, 0.0).astype(BF16)

    def heads(b):
        p = b.astype(BF16)
        return jnp.concatenate([jnp.where(m, p, jnp.zeros_like(p)) for m in head_masks], axis=0)

    groups = range(BRANCH // g4)
    lanes = [slice(grp * g4, (grp + 1) * g4) for grp in groups]
    rows = [slice(n * c, (n + 1) * c) for n in range(RW_STEP_CHUNKS)]
    units = [(rw, sl) for rw in rows for sl in lanes]

    def each(fn, *cols):
        return [fn(*args) for args in zip(*cols)]

    def prod(a, b, dims=_NN):
        m = a.shape[0]
        both = lax.dot_general(jnp.concatenate(_split(a, 2), axis=0), b, dims, preferred_element_type=F32)
        return both[:m] + both[m:]

    lw_all = lw_ref[...]
    cum_all = _mm([tri], _split(lw_all, 3))
    lw = [lw_all[rw, sl] for rw, sl in units]
    cum = [cum_all[rw, sl] for rw, sl in units]
    cum_end = [x[c - 1:c, :] for x in cum]
    kk = [kk_ref[rw, sl] for rw, sl in units]
    kka = [x * a_ref[rw, sl] for x, (rw, sl) in zip(kk, units)]
    k = [k_ref[rw, sl] for rw, sl in units]
    v = [v_ref[rw, sl] for rw, sl in units]
    g_inv = each(lambda x: jnp.exp(-x), cum)
    tail = each(lambda e, x: jnp.exp(e - x), cum_end, cum)
    alp = each(lambda x, cm, l: x * jnp.exp(cm - l), kk, cum, lw)
    rho = each(lambda un, cm: r_ref[un[0], un[1]] * jnp.exp(cm), units, cum)
    ar = each(lambda x, y: jnp.concatenate([x, y], axis=0), alp, rho)
    pb = each(lambda x, y, gi: prod(x, heads(y * gi), _NT), ar, kka, g_inv)
    pk = each(lambda x, y, gi: prod(x, heads(y * gi), _NT), ar, k, g_inv)
    l_b = each(lambda p: jnp.where(strict, p[:c], 0.0), pb)
    m_b = each(lambda p: jnp.where(incl, p[c:], 0.0), pb)
    lm_k = each(lambda p: jnp.concatenate([jnp.where(strict, p[:c], 0.0), jnp.where(incl, p[c:], 0.0)], axis=0), pk)

    nb = RW_NEUMANN
    eye = jnp.where(ci == ti, 1.0, 0.0)
    l1 = each(lambda ll: jnp.where((ci // nb) == (ti // nb), ll, 0.0), l_b)
    l2 = each(lambda ll: prod(ll, heads(ll)), l1)
    l34 = each(lambda ll, sq: prod(jnp.concatenate([ll, sq], axis=0), heads(sq)), l1, l2)
    a1 = each(lambda ll, sq, p: eye - ll + sq - p[:c], l1, l2, l34)
    x = each(lambda aa, p: aa + prod(aa, heads(p[c:])), a1, l34)
    m = nb
    while m < c:
        lower = ((ti // m) % 2 == 1) & ((ci // m) == (ti // m) - 1)
        y = each(lambda xx, ll: prod(xx, heads(jnp.where(lower, ll, 0.0))), x, l_b)
        x = each(lambda xx, yy: xx - prod(yy, heads(xx)), x, y)
        m *= 2

    lmv = each(lambda lm, vv: prod(lm, heads(vv)), lm_k, v)
    kb = each(lambda kx, ka, t: jnp.concatenate([kx * t, ka * t], axis=0).astype(BF16), k, kka, tail)
    g_end = each(jnp.exp, cum_end)

    st = [s_ref[grp] for grp in groups]
    ng = len(lanes)
    for n, rw in enumerate(rows):
        mine = slice(n * ng, (n + 1) * ng)
        ars = each(lambda a, s: prod(a, s.astype(BF16), _NT), ar[mine], st)
        u = each(lambda xx, a, p: prod(xx, heads(a[:c] + p[:c])), x[mine], ars, lmv[mine])
        o = each(lambda a, p, mb, uu: a[c:] + p[c:] - prod(mb, heads(uu)), ars, lmv[mine], m_b[mine], u)
        vu_t = each(lambda vv, uu: jnp.concatenate([vv, -uu], axis=0).T, v[mine], u)
        st = each(lambda s, e, a, b: jnp.where(bd_mask, s * e + prod(a, b), 0.0), st, g_end[mine], vu_t, kb[mine])
        for grp in groups:
            o_ref[rw, lanes[grp]] = o[grp]
    for grp in groups:
        s_ref[grp] = st[grp]


def _rwkv_recurrence(r, lw, k, v, kk, a):
    s, w = r.shape
    c = RW_STEP_CHUNKS * RW_CHUNK
    spec = pl.BlockSpec((c, w), lambda i: (i, 0))
    nbytes = 2 * 7 * c * w * 4 + (w // MXU_DIM) * MXU_DIM * MXU_DIM * 4 + RW_STEP_CHUNKS * 64 * MXU_DIM * MXU_DIM * 4
    return pl.pallas_call(
        _rwkv_kernel,
        grid=(s // c,),
        in_specs=[spec] * 6,
        out_specs=spec,
        out_shape=jax.ShapeDtypeStruct((s, w), F32),
        scratch_shapes=[pltpu.VMEM((w // MXU_DIM, MXU_DIM, MXU_DIM), F32)],
        compiler_params=_params(("arbitrary",), nbytes),
        name="rwkv_recurrence",
    )(r, lw, k, v, kk, a)


def _rwkv_post_kernel(wkv_ref, g_ref, bonus_ref, lnw_ref, lnb_ref, o_ref):
    ones_bd = _head_ones()
    x = wkv_ref[...]
    d = x - _head_sum(x, ones_bd, 3) * (1.0 / RW_N)
    var = _head_sum(d * d, ones_bd, 3) * (1.0 / RW_N)
    y = d * lax.rsqrt(var + RW_LN_EPS) * lnw_ref[...] + lnb_ref[...]
    o_ref[...] = ((y + bonus_ref[...]) * g_ref[...]).astype(o_ref.dtype)


def _rwkv_post(wkv, g, bonus, ln_w, ln_b, tm=512):
    s, w = wkv.shape
    tm = min(tm, s)
    blk = pl.BlockSpec((tm, w), lambda i: (i, 0))
    vec = pl.BlockSpec((1, w), lambda i: (0, 0))
    nbytes = 2 * 4 * tm * w * 4 + 8 * tm * w * 4
    return pl.pallas_call(
        _rwkv_post_kernel,
        grid=(s // tm,),
        in_specs=[blk, blk, blk, vec, vec],
        out_specs=blk,
        out_shape=jax.ShapeDtypeStruct((s, w), BF16),
        compiler_params=_params(("parallel",), nbytes),
        name="rwkv_post",
    )(wkv, g, bonus, ln_w, ln_b)


def _merge_kernel(oa_ref, ob_ref, oc_ref, wb_ref, ga_ref, gb_ref, gc_ref, o_ref):
    acc = None
    for o_g, gate, idx in ((oa_ref, ga_ref, 0), (ob_ref, gb_ref, 1), (oc_ref, gc_ref, 2)):
        u = jnp.dot(o_g[...], wb_ref[idx], preferred_element_type=F32)
        t = jax.nn.sigmoid(gate[...]) * u
        acc = t if acc is None else acc + t
    o_ref[...] = acc.astype(o_ref.dtype)


def _merge(oa, ob, oc, w_branch, layer, y, tm=1024, tn=512):
    s, w = oa.shape
    d = w_branch.shape[3]
    tm = min(tm, s)
    branch = pl.BlockSpec((tm, w), lambda i, j: (i, 0))

    def gate(gi):
        return pl.BlockSpec((tm, tn), lambda i, j: (i, (C_GATES + gi * d) // tn + j))

    nbytes = 2 * 3 * tm * w * 2 + 2 * 3 * w * tn * 2 + 2 * 4 * tm * tn * 4 + 4 * tm * tn * 4
    return pl.pallas_call(
        _merge_kernel,
        grid=(s // tm, d // tn),
        in_specs=[branch, branch, branch, pl.BlockSpec((None, 3, w, tn), lambda i, j: (layer, 0, 0, j)),
                  gate(0), gate(1), gate(2)],
        out_specs=pl.BlockSpec((tm, tn), lambda i, j: (i, j)),
        out_shape=jax.ShapeDtypeStruct((s, d), BF16),
        compiler_params=_params(("parallel", "arbitrary"), nbytes),
        name="merge",
    )(oa, ob, oc, w_branch, y, y, y)


def _out_proj_kernel(m_ref, w_ref, x_ref, g_ref, o_ref):
    mix = jnp.dot(m_ref[...], w_ref[...], preferred_element_type=F32)
    o_ref[...] = x_ref[...] + _rms(mix, g_ref[...])


def _out_proj(merged, w_out, layer, x, gain, tm=512):
    s, d = x.shape
    tm = min(tm, s)
    nbytes = 2 * tm * d * 2 + 2 * d * d * 2 + 4 * tm * d * 4 + 2 * tm * d * 4
    return pl.pallas_call(
        _out_proj_kernel,
        grid=(s // tm,),
        in_specs=[pl.BlockSpec((tm, d), lambda i: (i, 0)), pl.BlockSpec((None, d, d), lambda i: (layer, 0, 0)),
                  pl.BlockSpec((tm, d), lambda i: (i, 0)), pl.BlockSpec((1, d), lambda i: (0, 0))],
        out_specs=pl.BlockSpec((tm, d), lambda i: (i, 0)),
        out_shape=jax.ShapeDtypeStruct((s, d), F32),
        compiler_params=_params(("parallel",), nbytes),
        name="out_proj",
    )(merged, w_out, x, gain)


def _pad_cols(w, width):
    return jnp.pad(w, [(0, 0)] * (w.ndim - 1) + [(0, width - w.shape[-1])])


def _pad_rows(w, height, offset=0):
    return jnp.pad(w, ((offset, height - offset - w.shape[0]), (0, 0)))


def _pack_w_in(w):
    b = BRANCH
    gla0 = 3 * b
    alpha0 = gla0 + 2 * GLA_HEADS * GLA_DK + 2 * b
    rw0 = alpha0 + GLA_RANK
    wa0 = rw0 + 3 * b
    gl0 = wa0 + 2 * RW_RANK
    gates0 = gl0 + RW_GATE_RANK
    wt = jnp.swapaxes(w, 1, 2)

    def pad_to(t, rows):
        return jnp.pad(t, ((0, 0), (0, rows - t.shape[1]), (0, 0)))

    return jnp.concatenate([
        wt[:, :b] * (SB_HEAD_DIM ** -0.5), wt[:, b:3 * b],
        wt[:, gla0:alpha0], wt[:, rw0:wa0], wt[:, gates0:],
        pad_to(wt[:, alpha0:rw0], LANES), wt[:, wa0:gl0], pad_to(wt[:, gl0:gates0], 2 * LANES)],
        axis=1).astype(BF16)


def _token_mixing(x, gain, w_packed, layer, gla_gate_up, gla_gate_bias, gla_norm, rwkv_mu, rwkv_w_up, rwkv_w0,
                  rwkv_a_up, rwkv_a0, rwkv_g_up, rwkv_k_k, rwkv_k_a, rwkv_r_k, rwkv_ln_w, rwkv_ln_b,
                  w_branch, w_out, gain_post):
    b = BRANCH
    qkv = _norm_matmul(x, gain, w_packed, layer, 0, N_SB, BF16)
    y = _norm_matmul(x, gain, w_packed, layer, N_SB, N_REST, F32)

    o_a = _sb_attention(qkv)

    o_b = _gla(y, _pad_rows(gla_gate_up, LANES).astype(BF16), gla_gate_bias[None, :], gla_norm[None, :])

    mu = rwkv_mu
    row = lambda t: t[None, :]
    prep = _rwkv_prep(
        y, row(mu[:b]), row(mu[b:2 * b]), row(mu[2 * b:3 * b]), row(mu[3 * b:3 * b + 2 * RW_RANK]),
        _pad_cols(row(mu[3 * b + 2 * RW_RANK:]), 2 * LANES),
        _pad_rows(rwkv_w_up, LANES).astype(BF16), _pad_rows(rwkv_a_up, LANES, RW_RANK).astype(BF16),
        _pad_rows(rwkv_g_up, 2 * LANES).astype(BF16),
        row(rwkv_w0), row(rwkv_a0), row(rwkv_k_k), row(rwkv_k_a), row(rwkv_r_k.reshape(-1)))
    r, lw, k, v, kk, a, g, bonus = prep
    wkv = _rwkv_recurrence(r, lw, k, v, kk, a)
    o_c = _rwkv_post(wkv, g, bonus, row(rwkv_ln_w), row(rwkv_ln_b))

    merged = _merge(o_a, o_b, o_c, w_branch, layer, y)
    return _out_proj(merged, w_out, layer, x, gain_post)


def kernel(x, norm_pre, norm_post, ffn_in, ffn_out, w_in, gla_gate_up, gla_gate_bias, gla_norm, rwkv_mu,
           rwkv_w_up, rwkv_w0, rwkv_a_up, rwkv_a0, rwkv_g_up, rwkv_k_k, rwkv_k_a, rwkv_r_k, rwkv_ln_w,
           rwkv_ln_b, w_branch, w_out):
    batch, seq, d = x.shape
    assert batch == 1 and d == D_MODEL
    h = x[0]
    w_packed = _pack_w_in(w_in)
    w_branch, w_out = w_branch.astype(BF16), w_out.astype(BF16)
    for l in range(norm_pre.shape[0]):
        h = _ffn(h, norm_pre[l, 0][None, :], norm_post[l, 0][None, :], ffn_in, ffn_out, l, 0)
        h = _token_mixing(h, norm_pre[l, 1][None, :], w_packed, l, gla_gate_up[l], gla_gate_bias[l], gla_norm[l],
                          rwkv_mu[l], rwkv_w_up[l], rwkv_w0[l], rwkv_a_up[l], rwkv_a0[l], rwkv_g_up[l],
                          rwkv_k_k[l], rwkv_k_a[l], rwkv_r_k[l], rwkv_ln_w[l], rwkv_ln_b[l],
                          w_branch, w_out, norm_post[l, 1][None, :])
        h = _ffn(h, norm_pre[l, 2][None, :], norm_post[l, 2][None, :], ffn_in, ffn_out, l, 1)
    return h[None]
```

```python
import functools

import jax
import jax.numpy as jnp
from jax import lax
from jax.experimental import pallas as pl
from jax.experimental.pallas import tpu as pltpu

F32 = jnp.float32
BF16 = jnp.bfloat16

D_MODEL = 2048
BRANCH = D_MODEL // 2
FFN_SCALE = 0.5
NORM_EPS = 1e-6

SB_HEAD_DIM = 64
GLA_HEADS = 4
GLA_DV = BRANCH // GLA_HEADS
GLA_DK = GLA_DV // 2
GLA_RANK = 16
GLA_TAU = 16.0
GLA_CHUNK = 64
RW_N = 64
RW_RANK = 64
RW_GATE_RANK = 160
RW_LN_EPS = 64e-5

LANES = 128
MXU_DIM = 256
VMEM_LIMIT_CAP = 60000 * 1024

C_GLA_Q = 0
C_GLA_K = 512
C_GLA_V = 1024
C_GLA_R = 2048
C_RW_R = 3072
C_RW_K = 4096
C_RW_V = 5120
C_GATES = 6144
C_ALPHA = 12288
C_WA = 12416
C_GLOW = 12544
N_REST = 12800
N_SB = 3 * BRANCH


def _vmem_limit(nbytes):
    return int(min(max(2 * nbytes, 32 * 1024 * 1024), VMEM_LIMIT_CAP))


def _params(sem, nbytes):
    return pltpu.CompilerParams(dimension_semantics=sem, vmem_limit_bytes=_vmem_limit(nbytes))


def _rms(x, gain):
    ms = jnp.mean(x * x, axis=-1, keepdims=True)
    return x * lax.rsqrt(ms + NORM_EPS) * gain


def _softplus(x):
    return jnp.maximum(x, 0.0) + jnp.log(1.0 + jnp.exp(-jnp.abs(x)))


def _split(x, n):
    parts = []
    for _ in range(n - 1):
        p = x.astype(BF16)
        parts.append(p)
        x = x - p.astype(F32)
    parts.append(x.astype(BF16))
    return parts


_NN = (((1,), (0,)), ((), ()))
_NT = (((1,), (1,)), ((), ()))


def _mm(a_parts, b_parts, dims=_NN, order=3):
    acc = None
    for i, a in enumerate(a_parts):
        for j, b in enumerate(b_parts):
            if i + j < order:
                t = lax.dot_general(a, b, dims, preferred_element_type=F32)
                acc = t if acc is None else acc + t
    return acc


def _ffn_kernel(x_ref, gpre_ref, gpost_ref, wg_ref, wu_ref, wo_ref, o_ref, h_ref, acc_ref):
    j = pl.program_id(1)

    @pl.when(j == 0)
    def _():
        h_ref[...] = _rms(x_ref[...], gpre_ref[...]).astype(BF16)
        acc_ref[...] = jnp.zeros_like(acc_ref)

    h = h_ref[...]
    g = jnp.dot(h, wg_ref[...].astype(BF16), preferred_element_type=F32)
    u = jnp.dot(h, wu_ref[...].astype(BF16), preferred_element_type=F32)
    a = (g * jax.nn.sigmoid(g) * u).astype(BF16)
    acc_ref[...] += jnp.dot(a, wo_ref[...].astype(BF16), preferred_element_type=F32)

    @pl.when(j == pl.num_programs(1) - 1)
    def _():
        o_ref[...] = x_ref[...] + FFN_SCALE * _rms(acc_ref[...], gpost_ref[...])


def _ffn(x, gpre, gpost, w_in, w_out, layer, which, tm=1024, tf=256):
    s, d = x.shape
    f = w_out.shape[2]
    tm = min(tm, s)
    nj = f // tf
    nbytes = (2 * tm * d * 4) + tm * d * 6 + 2 * (3 * d * tf * 4) + 3 * d * tf * 2 + 4 * tm * tf * 4
    return pl.pallas_call(
        _ffn_kernel,
        grid=(s // tm, nj),
        in_specs=[
            pl.BlockSpec((tm, d), lambda i, j: (i, 0)),
            pl.BlockSpec((1, d), lambda i, j: (0, 0)),
            pl.BlockSpec((1, d), lambda i, j: (0, 0)),
            pl.BlockSpec((None, None, d, tf), lambda i, j: (layer, which, 0, j)),
            pl.BlockSpec((None, None, d, tf), lambda i, j: (layer, which, 0, j + nj)),
            pl.BlockSpec((None, None, tf, d), lambda i, j: (layer, which, j, 0)),
        ],
        out_specs=pl.BlockSpec((tm, d), lambda i, j: (i, 0), pipeline_mode=pl.Buffered(1)),
        out_shape=jax.ShapeDtypeStruct((s, d), F32),
        scratch_shapes=[pltpu.VMEM((tm, d), BF16), pltpu.VMEM((tm, d), F32)],
        compiler_params=_params(("parallel", "arbitrary"), nbytes),
        name="ffn",
    )(x, gpre, gpost, w_in, w_in, w_out)


def _norm_matmul_kernel(x_ref, g_ref, w_ref, o_ref, h_ref):
    @pl.when(pl.program_id(1) == 0)
    def _():
        h_ref[...] = _rms(x_ref[...], g_ref[...]).astype(BF16)

    o_ref[...] = lax.dot_general(h_ref[...], w_ref[...], _NT, preferred_element_type=F32).astype(o_ref.dtype)


def _norm_matmul(x, gain, w, layer, col0, n, out_dtype, tm=1024, tn=512):
    s, d = x.shape
    tm = min(tm, s)
    nbytes = 2 * tm * d * 4 + tm * d * 2 + 2 * d * tn * 2 + 2 * tm * tn * 4
    return pl.pallas_call(
        _norm_matmul_kernel,
        grid=(s // tm, n // tn),
        in_specs=[
            pl.BlockSpec((tm, d), lambda i, j: (i, 0)),
            pl.BlockSpec((1, d), lambda i, j: (0, 0)),
            pl.BlockSpec((None, tn, d), lambda i, j: (layer, col0 // tn + j, 0)),
        ],
        out_specs=pl.BlockSpec((tm, tn), lambda i, j: (i, j)),
        out_shape=jax.ShapeDtypeStruct((s, n), out_dtype),
        scratch_shapes=[pltpu.VMEM((tm, d), BF16)],
        compiler_params=_params(("parallel", "arbitrary"), nbytes),
        name="norm_matmul",
    )(x, gain, w)


SB_TILE = 256
SB_SUB = 4
SB_EXP_CLAMP = 40.0
SB_LOG_ZERO = -110.0
SB_NO_KEYS = -1e30


def _sb_kernel(q_ref, k_ref, v_ref, o_ref):
    t, nsub = SB_TILE, SB_SUB
    first = nsub * pl.program_id(1)
    left = lax.broadcasted_iota(jnp.int32, (1, LANES), 1) < SB_HEAD_DIM
    q_heads = []
    for s in range(nsub):
        q = q_ref[s * t:(s + 1) * t, :]
        qz = jnp.zeros_like(q)
        q_heads.append((jnp.where(left, q, qz), jnp.where(left, qz, q)))
    row = lax.broadcasted_iota(jnp.int32, (t, t), 0)
    col = lax.broadcasted_iota(jnp.int32, (t, t), 1)
    neg_tri = jnp.where(row >= col, -1.0, 0.0).astype(BF16)
    neg_tri2 = jnp.concatenate([neg_tri, neg_tri], axis=0)
    causal = col < row
    chains = [(s, hh) for s in range(nsub) for hh in range(2)]

    def step(back, diagonal, accs, cs):
        ks, vs = [], []
        for s in range(nsub):
            kb = first + s - back
            if not diagonal:
                cs = [jnp.where(kb >= 0, c, SB_NO_KEYS) if ch[0] == s else c for c, ch in zip(cs, chains)]
                kb = jnp.maximum(kb, 0)
            start = pl.multiple_of(kb * t, t)
            ks.append(k_ref[pl.ds(start, t), :])
            v = v_ref[pl.ds(start, t), :]
            vz = jnp.zeros_like(v)
            vs.append((jnp.where(left, v, vz), jnp.where(left, vz, v)))
        z = [lax.dot_general(q_heads[s][hh], ks[s], _NT, preferred_element_type=F32) for s, hh in chains]
        sp = [jnp.maximum(jnp.log(1.0 + jnp.exp(jnp.minimum(x, SB_EXP_CLAMP))), x) for x in z]
        if diagonal:
            sp = [jnp.where(causal, x, 0.0) for x in sp]
        cum = [jnp.dot(jnp.concatenate(_split(x, 2), axis=1), neg_tri2, preferred_element_type=F32) for x in sp]
        w = [jnp.exp(zz + cc + c) for zz, cc, c in zip(z, cum, cs)]
        if diagonal:
            w = [jnp.where(causal, x, 0.0) for x in w]
        pv = [jnp.dot(x.astype(BF16), vs[s][hh], preferred_element_type=F32) for x, (s, hh) in zip(w, chains)]
        accs = [accs[s] + pv[2 * s] + pv[2 * s + 1] for s in range(nsub)]
        cs = [c + cc[:, :1] for c, cc in zip(cs, cum)]
        live = jnp.max(functools.reduce(jnp.maximum, cs)) >= SB_LOG_ZERO
        return accs, cs, live.astype(jnp.int32)

    accs = [jnp.zeros((t, LANES), F32) for _ in range(nsub)]
    cs = [jnp.zeros((t, 1), F32) for _ in chains]
    accs, cs, live = step(0, True, accs, cs)

    def more(carry):
        return jnp.logical_and(carry[0] < first + nsub, carry[1] > 0)

    def walk(carry):
        accs, cs, live = step(carry[0], False, list(carry[2]), list(carry[3]))
        return carry[0] + 1, live, tuple(accs), tuple(cs)

    carry = lax.while_loop(more, walk, (jnp.int32(1), live, tuple(accs), tuple(cs)))
    for s in range(nsub):
        o_ref[s * t:(s + 1) * t, :] = carry[2][s].astype(o_ref.dtype)


def _sb_attention(qkv):
    s = qkv.shape[0]
    t = SB_TILE
    tq = SB_SUB * t
    npair = BRANCH // LANES
    nbytes = 2 * 2 * s * LANES * 2 + 16 * SB_SUB * t * t * 4
    return pl.pallas_call(
        _sb_kernel,
        grid=(npair, s // tq),
        in_specs=[
            pl.BlockSpec((tq, LANES), lambda p, i: (i, p)),
            pl.BlockSpec((s, LANES), lambda p, i: (0, npair + p)),
            pl.BlockSpec((s, LANES), lambda p, i: (0, 2 * npair + p)),
        ],
        out_specs=pl.BlockSpec((tq, LANES), lambda p, i: (i, p)),
        out_shape=jax.ShapeDtypeStruct((s, BRANCH), BF16),
        compiler_params=_params(("parallel", "arbitrary"), nbytes),
        name="sb_attention",
    )(qkv, qkv, qkv)


GLA_TILE = 256


def _gla_kernel(q_ref, k_ref, v_ref, r_ref, al_ref, gup_ref, gb_ref, gn_ref, o_ref, st_ref):
    t, c, dk, dv = GLA_TILE, GLA_CHUNK, GLA_DK, GLA_DV

    @pl.when(pl.program_id(0) == 0)
    def _():
        st_ref[...] = jnp.zeros_like(st_ref)

    x = jnp.dot(al_ref[...].astype(BF16), gup_ref[...], preferred_element_type=F32) + gb_ref[...]
    g = -_softplus(-x) * (1.0 / GLA_TAU)
    row = lax.broadcasted_iota(jnp.int32, (t, t), 0)
    col = lax.broadcasted_iota(jnp.int32, (t, t), 1)
    same = (row // c) == (col // c)
    tril = same & (col <= row)
    g_parts = _split(g, 3)
    b_all = _mm([jnp.where(tril, 1.0, 0.0).astype(BF16)], g_parts)
    last_all = _mm([jnp.where(same, 1.0, 0.0).astype(BF16)], g_parts)
    hs = range(GLA_HEADS)
    kcols = [slice(h * dk, (h + 1) * dk) for h in hs]
    vcols = [slice(h * dv, (h + 1) * dv) for h in hs]
    b = [b_all[:, sl] for sl in kcols]
    b_last = [last_all[:, sl] for sl in kcols]
    k = [k_ref[:, sl] for sl in kcols]
    v = [v_ref[:, sl].astype(BF16) for sl in vcols]
    q_dec = [(q_ref[:, sl] * (dk ** -0.5) * jnp.exp(bb)).astype(BF16) for sl, bb in zip(kcols, b)]
    k_dec = [(kk * jnp.exp(-bb)).astype(BF16) for kk, bb in zip(k, b)]
    k_tail = [(kk * jnp.exp(bl - bb)).astype(BF16) for kk, bl, bb in zip(k, b_last, b)]
    chunk_decay = [jnp.exp(bl) for bl in b_last]
    scores = [lax.dot_general(qd, kd, _NT, preferred_element_type=F32) for qd, kd in zip(q_dec, k_dec)]
    scores = [jnp.where(tril, sc, 0.0).astype(BF16) for sc in scores]
    o_intra = [jnp.dot(sc, vv, preferred_element_type=F32) for sc, vv in zip(scores, v)]
    v_t = [vv.astype(F32).T.astype(BF16) for vv in v]
    st = [st_ref[h] for h in hs]
    outs = [[] for _ in hs]
    for n in range(t // c):
        rows = slice(n * c, (n + 1) * c)
        o_inter = [lax.dot_general(qd[rows], s.astype(BF16), _NT, preferred_element_type=F32) for qd, s in zip(q_dec, st)]
        for h in hs:
            outs[h].append(o_intra[h][rows] + o_inter[h])
        kv = [jnp.dot(vt[:, rows], kt[rows], preferred_element_type=F32) for vt, kt in zip(v_t, k_tail)]
        st = [s * cd[n * c:n * c + 1, :] + x for s, cd, x in zip(st, chunk_decay, kv)]
    for h in hs:
        st_ref[h] = st[h]
        o = jnp.concatenate(outs[h], axis=0)
        o = o * lax.rsqrt(jnp.mean(o * o, axis=-1, keepdims=True) + NORM_EPS)
        r = r_ref[:, vcols[h]]
        o_ref[:, vcols[h]] = (o * gn_ref[:, vcols[h]] * (r * jax.nn.sigmoid(r))).astype(o_ref.dtype)


def _gla(y, gate_up, gate_bias, gla_norm):
    s = y.shape[0]
    t = min(GLA_TILE, s)
    kw, vw = GLA_HEADS * GLA_DK, GLA_HEADS * GLA_DV
    nbytes = 2 * t * (2 * kw + 3 * vw + LANES) * 4 + GLA_HEADS * 12 * t * t * 4
    return pl.pallas_call(
        _gla_kernel,
        grid=(s // t,),
        in_specs=[
            pl.BlockSpec((t, kw), lambda i: (i, C_GLA_Q // kw)),
            pl.BlockSpec((t, kw), lambda i: (i, C_GLA_K // kw)),
            pl.BlockSpec((t, vw), lambda i: (i, C_GLA_V // vw)),
            pl.BlockSpec((t, vw), lambda i: (i, C_GLA_R // vw)),
            pl.BlockSpec((t, LANES), lambda i: (i, C_ALPHA // LANES)),
            pl.BlockSpec((LANES, kw), lambda i: (0, 0)),
            pl.BlockSpec((1, kw), lambda i: (0, 0)),
            pl.BlockSpec((1, vw), lambda i: (0, 0)),
        ],
        out_specs=pl.BlockSpec((t, vw), lambda i: (i, 0)),
        out_shape=jax.ShapeDtypeStruct((s, BRANCH), BF16),
        scratch_shapes=[pltpu.VMEM((GLA_HEADS, GLA_DV, GLA_DK), F32)],
        compiler_params=_params(("arbitrary",), nbytes),
        name="gla",
    )(y, y, y, y, y, gate_up, gate_bias, gla_norm)


def _head_ones():
    r = lax.broadcasted_iota(jnp.int32, (MXU_DIM, MXU_DIM), 0) // RW_N
    c = lax.broadcasted_iota(jnp.int32, (MXU_DIM, MXU_DIM), 1) // RW_N
    return jnp.where(r == c, 1.0, 0.0).astype(BF16)


def _head_sum(x, ones_bd, n_split):
    outs = []
    for i in range(x.shape[1] // MXU_DIM):
        outs.append(_mm(_split(x[:, i * MXU_DIM:(i + 1) * MXU_DIM], n_split), [ones_bd], order=n_split))
    return jnp.concatenate(outs, axis=1)


def _rwkv_prep_kernel(yr_ref, yk_ref, yv_ref, ywa_ref, yg_ref, pr_ref, pk_ref, pv_ref, pwa_ref, pg_ref,
                      mur_ref, muk_ref, muv_ref, muwa_ref, mug_ref, wup_ref, aup_ref, gup_ref,
                      w0_ref, a0_ref, kk_ref, ka_ref, rk_ref,
                      r_out, lw_out, k_out, v_out, kk_out, a_out, g_out, bonus_out):
    first = pl.program_id(0) == 0

    def shifted(y_ref, p_ref, mu_ref):
        y = y_ref[...]
        prev_row = jnp.where(first, 0.0, p_ref[7:8, :])
        row = lax.broadcasted_iota(jnp.int32, y.shape, 0)
        y_prev = jnp.where(row == 0, prev_row, pltpu.roll(y, 1, axis=0))
        return y + (y_prev - y) * mu_ref[...]

    r = shifted(yr_ref, pr_ref, mur_ref)
    k = shifted(yk_ref, pk_ref, muk_ref)
    v = shifted(yv_ref, pv_ref, muv_ref)
    wa = shifted(ywa_ref, pwa_ref, muwa_ref)
    gl = shifted(yg_ref, pg_ref, mug_ref)
    w_pre = w0_ref[...] + jnp.dot(jnp.tanh(wa).astype(BF16), wup_ref[...], preferred_element_type=F32)
    lw = -jnp.exp(-_softplus(-w_pre) - 0.5)
    a = jax.nn.sigmoid(a0_ref[...] + jnp.dot(wa.astype(BF16), aup_ref[...], preferred_element_type=F32))
    g = jnp.dot(jax.nn.sigmoid(gl).astype(BF16), gup_ref[...], preferred_element_type=F32)
    ones_bd = _head_ones()
    kk = k * kk_ref[...]
    kk = kk / jnp.maximum(jnp.sqrt(_head_sum(kk * kk, ones_bd, 3)), 1e-12)
    k = k * (1.0 + (a - 1.0) * ka_ref[...])
    bonus = _head_sum(r * k * rk_ref[...], ones_bd, 3) * v
    r_out[...] = r
    lw_out[...] = lw
    k_out[...] = k
    v_out[...] = v
    kk_out[...] = kk
    a_out[...] = a
    g_out[...] = g
    bonus_out[...] = bonus


def _rwkv_prep(y, mu_r, mu_k, mu_v, mu_wa, mu_g, wup, aup, gup, w0, a0, k_k, k_a, r_k, tm=512):
    s = y.shape[0]
    tm = min(tm, s)
    w = BRANCH

    def cur(width, col):
        return pl.BlockSpec((tm, width), lambda i: (i, col // width))

    def prev(width, col):
        return pl.BlockSpec((8, width), lambda i: (jnp.maximum(i * (tm // 8) - 1, 0), col // width))

    def full(shape):
        return pl.BlockSpec(shape, lambda i: (0, 0))

    nbytes = 2 * tm * (3 * w + 384) * 4 + 2 * 8 * tm * w * 4 + 12 * tm * w * 4
    out = jax.ShapeDtypeStruct((s, w), F32)
    return pl.pallas_call(
        _rwkv_prep_kernel,
        grid=(s // tm,),
        in_specs=[cur(w, C_RW_R), cur(w, C_RW_K), cur(w, C_RW_V), cur(LANES, C_WA), cur(2 * LANES, C_GLOW),
                  prev(w, C_RW_R), prev(w, C_RW_K), prev(w, C_RW_V), prev(LANES, C_WA), prev(2 * LANES, C_GLOW),
                  full((1, w)), full((1, w)), full((1, w)), full((1, LANES)), full((1, 2 * LANES)),
                  full((LANES, w)), full((LANES, w)), full((2 * LANES, w)),
                  full((1, w)), full((1, w)), full((1, w)), full((1, w)), full((1, w))],
        out_specs=[pl.BlockSpec((tm, w), lambda i: (i, 0))] * 8,
        out_shape=[out] * 8,
        compiler_params=_params(("parallel",), nbytes),
        name="rwkv_prep",
    )(y, y, y, y, y, y, y, y, y, y, mu_r, mu_k, mu_v, mu_wa, mu_g, wup, aup, gup, w0, a0, k_k, k_a, r_k)


RW_CHUNK = 64
RW_PACK = MXU_DIM // RW_N
RW_NEUMANN = 8
RW_STEP_CHUNKS = 4


def _rwkv_kernel(r_ref, lw_ref, k_ref, v_ref, kk_ref, a_ref, o_ref, s_ref):
    c, g4 = RW_CHUNK, MXU_DIM

    @pl.when(pl.program_id(0) == 0)
    def _():
        s_ref[...] = jnp.zeros_like(s_ref)

    ti = lax.broadcasted_iota(jnp.int32, (c, g4), 0)
    ci = lax.broadcasted_iota(jnp.int32, (c, g4), 1) % c
    strict = ci < ti
    incl = ci <= ti
    lane_head = lax.broadcasted_iota(jnp.int32, (1, g4), 1) // RW_N
    head_masks = [lane_head == h for h in range(RW_PACK)]
    bd_mask = (lax.broadcasted_iota(jnp.int32, (g4, g4), 0) // RW_N) == (lax.broadcasted_iota(jnp.int32, (g4, g4), 1) // RW_N)
    tr = lax.broadcasted_iota(jnp.int32, (RW_STEP_CHUNKS * c, RW_STEP_CHUNKS * c), 0)
    tc = lax.broadcasted_iota(jnp.int32, (RW_STEP_CHUNKS * c, RW_STEP_CHUNKS * c), 1)
    tri = jnp.where((tc <= tr) & (tc // c == tr // c), 1.0, 0.0).astype(BF16)

    def heads(b):
        p = b.astype(BF16)
        return jnp.concatenate([jnp.where(m, p, jnp.zeros_like(p)) for m in head_masks], axis=0)

    groups = range(BRANCH // g4)
    lanes = [slice(grp * g4, (grp + 1) * g4) for grp in groups]
    rows = [slice(n * c, (n + 1) * c) for n in range(RW_STEP_CHUNKS)]
    units = [(rw, sl) for rw in rows for sl in lanes]

    def each(fn, *cols):
        return [fn(*args) for args in zip(*cols)]

    def prod(a, b, dims=_NN):
        m = a.shape[0]
        both = lax.dot_general(jnp.concatenate(_split(a, 2), axis=0), b, dims, preferred_element_type=F32)
        return both[:m] + both[m:]

    lw_all = lw_ref[...]
    cum_all = _mm([tri], _split(lw_all, 3))
    lw = [lw_all[rw, sl] for rw, sl in units]
    cum = [cum_all[rw, sl] for rw, sl in units]
    cum_end = [x[c - 1:c, :] for x in cum]
    kk = [kk_ref[rw, sl] for rw, sl in units]
    kka = [x * a_ref[rw, sl] for x, (rw, sl) in zip(kk, units)]
    k = [k_ref[rw, sl] for rw, sl in units]
    v = [v_ref[rw, sl] for rw, sl in units]
    g_inv = each(lambda x: jnp.exp(-x), cum)
    tail = each(lambda e, x: jnp.exp(e - x), cum_end, cum)
    alp = each(lambda x, cm, l: x * jnp.exp(cm - l), kk, cum, lw)
    rho = each(lambda un, cm: r_ref[un[0], un[1]] * jnp.exp(cm), units, cum)
    ar = each(lambda x, y: jnp.concatenate([x, y], axis=0), alp, rho)
    pb = each(lambda x, y, gi: prod(x, heads(y * gi), _NT), ar, kka, g_inv)
    pk = each(lambda x, y, gi: prod(x, heads(y * gi), _NT), ar, k, g_inv)
    l_b = each(lambda p: jnp.where(strict, p[:c], 0.0), pb)
    m_b = each(lambda p: jnp.where(incl, p[c:], 0.0), pb)
    lm_k = each(lambda p: jnp.concatenate([jnp.where(strict, p[:c], 0.0), jnp.where(incl, p[c:], 0.0)], axis=0), pk)

    nb = RW_NEUMANN
    eye = jnp.where(ci == ti, 1.0, 0.0)
    l1 = each(lambda ll: jnp.where((ci // nb) == (ti // nb), ll, 0.0), l_b)
    l2 = each(lambda ll: prod(ll, heads(ll)), l1)
    l34 = each(lambda ll, sq: prod(jnp.concatenate([ll, sq], axis=0), heads(sq)), l1, l2)
    a1 = each(lambda ll, sq, p: eye - ll + sq - p[:c], l1, l2, l34)
    x = each(lambda aa, p: aa + prod(aa, heads(p[c:])), a1, l34)
    m = nb
    while m < c:
        lower = ((ti // m) % 2 == 1) & ((ci // m) == (ti // m) - 1)
        y = each(lambda xx, ll: prod(xx, heads(jnp.where(lower, ll, 0.0))), x, l_b)
        x = each(lambda xx, yy: xx - prod(yy, heads(xx)), x, y)
        m *= 2

    lmv = each(lambda lm, vv: prod(lm, heads(vv)), lm_k, v)
    kb = each(lambda kx, ka, t: jnp.concatenate([kx * t, ka * t], axis=0).astype(BF16), k, kka, tail)
    g_end = each(jnp.exp, cum_end)

    st = [s_ref[grp] for grp in groups]
    ng = len(lanes)
    for n, rw in enumerate(rows):
        mine = slice(n * ng, (n + 1) * ng)
        ars = each(lambda a, s: prod(a, s.astype(BF16), _NT), ar[mine], st)
        u = each(lambda xx, a, p: prod(xx, heads(a[:c] + p[:c])), x[mine], ars, lmv[mine])
        o = each(lambda a, p, mb, uu: a[c:] + p[c:] - prod(mb, heads(uu)), ars, lmv[mine], m_b[mine], u)
        vu_t = each(lambda vv, uu: jnp.concatenate([vv, -uu], axis=0).T, v[mine], u)
        st = each(lambda s, e, a, b: jnp.where(bd_mask, s * e + prod(a, b), 0.0), st, g_end[mine], vu_t, kb[mine])
        for grp in groups:
            o_ref[rw, lanes[grp]] = o[grp]
    for grp in groups:
        s_ref[grp] = st[grp]


def _rwkv_recurrence(r, lw, k, v, kk, a):
    s, w = r.shape
    c = RW_STEP_CHUNKS * RW_CHUNK
    spec = pl.BlockSpec((c, w), lambda i: (i, 0))
    nbytes = 2 * 7 * c * w * 4 + (w // MXU_DIM) * MXU_DIM * MXU_DIM * 4 + RW_STEP_CHUNKS * 64 * MXU_DIM * MXU_DIM * 4
    return pl.pallas_call(
        _rwkv_kernel,
        grid=(s // c,),
        in_specs=[spec] * 6,
        out_specs=spec,
        out_shape=jax.ShapeDtypeStruct((s, w), F32),
        scratch_shapes=[pltpu.VMEM((w // MXU_DIM, MXU_DIM, MXU_DIM), F32)],
        compiler_params=_params(("arbitrary",), nbytes),
        name="rwkv_recurrence",
    )(r, lw, k, v, kk, a)


def _rwkv_post_kernel(wkv_ref, g_ref, bonus_ref, lnw_ref, lnb_ref, o_ref):
    ones_bd = _head_ones()
    x = wkv_ref[...]
    d = x - _head_sum(x, ones_bd, 3) * (1.0 / RW_N)
    var = _head_sum(d * d, ones_bd, 3) * (1.0 / RW_N)
    y = d * lax.rsqrt(var + RW_LN_EPS) * lnw_ref[...] + lnb_ref[...]
    o_ref[...] = ((y + bonus_ref[...]) * g_ref[...]).astype(o_ref.dtype)


def _rwkv_post(wkv, g, bonus, ln_w, ln_b, tm=512):
    s, w = wkv.shape
    tm = min(tm, s)
    blk = pl.BlockSpec((tm, w), lambda i: (i, 0))
    vec = pl.BlockSpec((1, w), lambda i: (0, 0))
    nbytes = 2 * 4 * tm * w * 4 + 8 * tm * w * 4
    return pl.pallas_call(
        _rwkv_post_kernel,
        grid=(s // tm,),
        in_specs=[blk, blk, blk, vec, vec],
        out_specs=blk,
        out_shape=jax.ShapeDtypeStruct((s, w), BF16),
        compiler_params=_params(("parallel",), nbytes),
        name="rwkv_post",
    )(wkv, g, bonus, ln_w, ln_b)


def _merge_kernel(oa_ref, ob_ref, oc_ref, wb_ref, ga_ref, gb_ref, gc_ref, o_ref):
    acc = None
    for o_g, gate, idx in ((oa_ref, ga_ref, 0), (ob_ref, gb_ref, 1), (oc_ref, gc_ref, 2)):
        u = jnp.dot(o_g[...], wb_ref[idx], preferred_element_type=F32)
        t = jax.nn.sigmoid(gate[...]) * u
        acc = t if acc is None else acc + t
    o_ref[...] = acc.astype(o_ref.dtype)


def _merge(oa, ob, oc, w_branch, layer, y, tm=1024, tn=512):
    s, w = oa.shape
    d = w_branch.shape[3]
    tm = min(tm, s)
    branch = pl.BlockSpec((tm, w), lambda i, j: (i, 0))

    def gate(gi):
        return pl.BlockSpec((tm, tn), lambda i, j: (i, (C_GATES + gi * d) // tn + j))

    nbytes = 2 * 3 * tm * w * 2 + 2 * 3 * w * tn * 2 + 2 * 4 * tm * tn * 4 + 4 * tm * tn * 4
    return pl.pallas_call(
        _merge_kernel,
        grid=(s // tm, d // tn),
        in_specs=[branch, branch, branch, pl.BlockSpec((None, 3, w, tn), lambda i, j: (layer, 0, 0, j)),
                  gate(0), gate(1), gate(2)],
        out_specs=pl.BlockSpec((tm, tn), lambda i, j: (i, j)),
        out_shape=jax.ShapeDtypeStruct((s, d), BF16),
        compiler_params=_params(("parallel", "arbitrary"), nbytes),
        name="merge",
    )(oa, ob, oc, w_branch, y, y, y)


def _out_proj_kernel(m_ref, w_ref, x_ref, g_ref, o_ref):
    mix = jnp.dot(m_ref[...], w_ref[...], preferred_element_type=F32)
    o_ref[...] = x_ref[...] + _rms(mix, g_ref[...])


def _out_proj(merged, w_out, layer, x, gain, tm=512):
    s, d = x.shape
    tm = min(tm, s)
    nbytes = 2 * tm * d * 2 + 2 * d * d * 2 + 4 * tm * d * 4 + 2 * tm * d * 4
    return pl.pallas_call(
        _out_proj_kernel,
        grid=(s // tm,),
        in_specs=[pl.BlockSpec((tm, d), lambda i: (i, 0)), pl.BlockSpec((None, d, d), lambda i: (layer, 0, 0)),
                  pl.BlockSpec((tm, d), lambda i: (i, 0)), pl.BlockSpec((1, d), lambda i: (0, 0))],
        out_specs=pl.BlockSpec((tm, d), lambda i: (i, 0)),
        out_shape=jax.ShapeDtypeStruct((s, d), F32),
        compiler_params=_params(("parallel",), nbytes),
        name="out_proj",
    )(merged, w_out, x, gain)


def _pad_cols(w, width):
    return jnp.pad(w, [(0, 0)] * (w.ndim - 1) + [(0, width - w.shape[-1])])


def _pad_rows(w, height, offset=0):
    return jnp.pad(w, ((offset, height - offset - w.shape[0]), (0, 0)))


def _pack_w_in(w):
    b = BRANCH
    gla0 = 3 * b
    alpha0 = gla0 + 2 * GLA_HEADS * GLA_DK + 2 * b
    rw0 = alpha0 + GLA_RANK
    wa0 = rw0 + 3 * b
    gl0 = wa0 + 2 * RW_RANK
    gates0 = gl0 + RW_GATE_RANK
    wt = jnp.swapaxes(w, 1, 2)

    def pad_to(t, rows):
        return jnp.pad(t, ((0, 0), (0, rows - t.shape[1]), (0, 0)))

    return jnp.concatenate([
        wt[:, :b] * (SB_HEAD_DIM ** -0.5), wt[:, b:3 * b],
        wt[:, gla0:alpha0], wt[:, rw0:wa0], wt[:, gates0:],
        pad_to(wt[:, alpha0:rw0], LANES), wt[:, wa0:gl0], pad_to(wt[:, gl0:gates0], 2 * LANES)],
        axis=1).astype(BF16)


def _token_mixing(x, gain, w_packed, layer, gla_gate_up, gla_gate_bias, gla_norm, rwkv_mu, rwkv_w_up, rwkv_w0,
                  rwkv_a_up, rwkv_a0, rwkv_g_up, rwkv_k_k, rwkv_k_a, rwkv_r_k, rwkv_ln_w, rwkv_ln_b,
                  w_branch, w_out, gain_post):
    b = BRANCH
    qkv = _norm_matmul(x, gain, w_packed, layer, 0, N_SB, BF16)
    y = _norm_matmul(x, gain, w_packed, layer, N_SB, N_REST, F32)

    o_a = _sb_attention(qkv)

    o_b = _gla(y, _pad_rows(gla_gate_up, LANES).astype(BF16), gla_gate_bias[None, :], gla_norm[None, :])

    mu = rwkv_mu
    row = lambda t: t[None, :]
    prep = _rwkv_prep(
        y, row(mu[:b]), row(mu[b:2 * b]), row(mu[2 * b:3 * b]), row(mu[3 * b:3 * b + 2 * RW_RANK]),
        _pad_cols(row(mu[3 * b + 2 * RW_RANK:]), 2 * LANES),
        _pad_rows(rwkv_w_up, LANES).astype(BF16), _pad_rows(rwkv_a_up, LANES, RW_RANK).astype(BF16),
        _pad_rows(rwkv_g_up, 2 * LANES).astype(BF16),
        row(rwkv_w0), row(rwkv_a0), row(rwkv_k_k), row(rwkv_k_a), row(rwkv_r_k.reshape(-1)))
    r, lw, k, v, kk, a, g, bonus = prep
    wkv = _rwkv_recurrence(r, lw, k, v, kk, a)
    o_c = _rwkv_post(wkv, g, bonus, row(rwkv_ln_w), row(rwkv_ln_b))

    merged = _merge(o_a, o_b, o_c, w_branch, layer, y)
    return _out_proj(merged, w_out, layer, x, gain_post)


def kernel(x, norm_pre, norm_post, ffn_in, ffn_out, w_in, gla_gate_up, gla_gate_bias, gla_norm, rwkv_mu,
           rwkv_w_up, rwkv_w0, rwkv_a_up, rwkv_a0, rwkv_g_up, rwkv_k_k, rwkv_k_a, rwkv_r_k, rwkv_ln_w,
           rwkv_ln_b, w_branch, w_out):
    batch, seq, d = x.shape
    assert batch == 1 and d == D_MODEL
    h = x[0]
    w_packed = _pack_w_in(w_in)
    w_branch, w_out = w_branch.astype(BF16), w_out.astype(BF16)
    for l in range(norm_pre.shape[0]):
        h = _ffn(h, norm_pre[l, 0][None, :], norm_post[l, 0][None, :], ffn_in, ffn_out, l, 0)
        h = _token_mixing(h, norm_pre[l, 1][None, :], w_packed, l, gla_gate_up[l], gla_gate_bias[l], gla_norm[l],
                          rwkv_mu[l], rwkv_w_up[l], rwkv_w0[l], rwkv_a_up[l], rwkv_a0[l], rwkv_g_up[l],
                          rwkv_k_k[l], rwkv_k_a[l], rwkv_r_k[l], rwkv_ln_w[l], rwkv_ln_b[l],
                          w_branch, w_out, norm_post[l, 1][None, :])
        h = _ffn(h, norm_pre[l, 2][None, :], norm_post[l, 2][None, :], ffn_in, ffn_out, l, 1)
    return h[None]
```

```python
import functools

import jax
import jax.numpy as jnp
from jax import lax
from jax.experimental import pallas as pl
from jax.experimental.pallas import tpu as pltpu

F32 = jnp.float32
BF16 = jnp.bfloat16

D_MODEL = 2048
BRANCH = D_MODEL // 2
FFN_SCALE = 0.5
NORM_EPS = 1e-6

SB_HEAD_DIM = 64
GLA_HEADS = 4
GLA_DV = BRANCH // GLA_HEADS
GLA_DK = GLA_DV // 2
GLA_RANK = 16
GLA_TAU = 16.0
GLA_CHUNK = 64
RW_N = 64
RW_RANK = 64
RW_GATE_RANK = 160
RW_LN_EPS = 64e-5

LANES = 128
MXU_DIM = 256
VMEM_LIMIT_CAP = 60000 * 1024

C_GLA_Q = 0
C_GLA_K = 512
C_GLA_V = 1024
C_GLA_R = 2048
C_RW_R = 3072
C_RW_K = 4096
C_RW_V = 5120
C_GATES = 6144
C_ALPHA = 12288
C_WA = 12416
C_GLOW = 12544
N_REST = 12800
N_SB = 3 * BRANCH


def _vmem_limit(nbytes):
    return int(min(max(2 * nbytes, 32 * 1024 * 1024), VMEM_LIMIT_CAP))


def _params(sem, nbytes):
    return pltpu.CompilerParams(dimension_semantics=sem, vmem_limit_bytes=_vmem_limit(nbytes))


def _rms(x, gain):
    ms = jnp.mean(x * x, axis=-1, keepdims=True)
    return x * lax.rsqrt(ms + NORM_EPS) * gain


def _softplus(x):
    return jnp.maximum(x, 0.0) + jnp.log(1.0 + jnp.exp(-jnp.abs(x)))


def _split(x, n):
    parts = []
    for _ in range(n - 1):
        p = x.astype(BF16)
        parts.append(p)
        x = x - p.astype(F32)
    parts.append(x.astype(BF16))
    return parts


_NN = (((1,), (0,)), ((), ()))
_NT = (((1,), (1,)), ((), ()))


def _mm(a_parts, b_parts, dims=_NN, order=3):
    acc = None
    for i, a in enumerate(a_parts):
        for j, b in enumerate(b_parts):
            if i + j < order:
                t = lax.dot_general(a, b, dims, preferred_element_type=F32)
                acc = t if acc is None else acc + t
    return acc


def _ffn_kernel(x_ref, gpre_ref, gpost_ref, wg_ref, wu_ref, wo_ref, o_ref, h_ref, acc_ref):
    j = pl.program_id(1)

    @pl.when(j == 0)
    def _():
        h_ref[...] = _rms(x_ref[...], gpre_ref[...]).astype(BF16)
        acc_ref[...] = jnp.zeros_like(acc_ref)

    h = h_ref[...]
    g = jnp.dot(h, wg_ref[...].astype(BF16), preferred_element_type=F32)
    u = jnp.dot(h, wu_ref[...].astype(BF16), preferred_element_type=F32)
    a = (g * jax.nn.sigmoid(g) * u).astype(BF16)
    acc_ref[...] += jnp.dot(a, wo_ref[...].astype(BF16), preferred_element_type=F32)

    @pl.when(j == pl.num_programs(1) - 1)
    def _():
        o_ref[...] = x_ref[...] + FFN_SCALE * _rms(acc_ref[...], gpost_ref[...])


def _ffn(x, gpre, gpost, w_in, w_out, layer, which, tm=1024, tf=256):
    s, d = x.shape
    f = w_out.shape[2]
    tm = min(tm, s)
    nj = f // tf
    nbytes = (2 * tm * d * 4) + tm * d * 6 + 2 * (3 * d * tf * 4) + 3 * d * tf * 2 + 4 * tm * tf * 4
    return pl.pallas_call(
        _ffn_kernel,
        grid=(s // tm, nj),
        in_specs=[
            pl.BlockSpec((tm, d), lambda i, j: (i, 0)),
            pl.BlockSpec((1, d), lambda i, j: (0, 0)),
            pl.BlockSpec((1, d), lambda i, j: (0, 0)),
            pl.BlockSpec((None, None, d, tf), lambda i, j: (layer, which, 0, j)),
            pl.BlockSpec((None, None, d, tf), lambda i, j: (layer, which, 0, j + nj)),
            pl.BlockSpec((None, None, tf, d), lambda i, j: (layer, which, j, 0)),
        ],
        out_specs=pl.BlockSpec((tm, d), lambda i, j: (i, 0), pipeline_mode=pl.Buffered(1)),
        out_shape=jax.ShapeDtypeStruct((s, d), F32),
        scratch_shapes=[pltpu.VMEM((tm, d), BF16), pltpu.VMEM((tm, d), F32)],
        compiler_params=_params(("parallel", "arbitrary"), nbytes),
        name="ffn",
    )(x, gpre, gpost, w_in, w_in, w_out)


def _norm_matmul_kernel(x_ref, g_ref, w_ref, o_ref, h_ref):
    @pl.when(pl.program_id(1) == 0)
    def _():
        h_ref[...] = _rms(x_ref[...], g_ref[...]).astype(BF16)

    o_ref[...] = lax.dot_general(h_ref[...], w_ref[...], _NT, preferred_element_type=F32).astype(o_ref.dtype)


def _norm_matmul(x, gain, w, layer, col0, n, out_dtype, tm=2048, tn=512):
    s, d = x.shape
    tm = min(tm, s)
    nbytes = 2 * tm * d * 4 + tm * d * 2 + 2 * d * tn * 2 + 2 * tm * tn * 4
    return pl.pallas_call(
        _norm_matmul_kernel,
        grid=(s // tm, n // tn),
        in_specs=[
            pl.BlockSpec((tm, d), lambda i, j: (i, 0)),
            pl.BlockSpec((1, d), lambda i, j: (0, 0)),
            pl.BlockSpec((None, tn, d), lambda i, j: (layer, col0 // tn + j, 0)),
        ],
        out_specs=pl.BlockSpec((tm, tn), lambda i, j: (i, j)),
        out_shape=jax.ShapeDtypeStruct((s, n), out_dtype),
        scratch_shapes=[pltpu.VMEM((tm, d), BF16)],
        compiler_params=_params(("parallel", "arbitrary"), nbytes),
        name="norm_matmul",
    )(x, gain, w)


SB_TILE = 256
SB_SUB = 4
SB_EXP_CLAMP = 40.0
SB_LOG_ZERO = -110.0
SB_NO_KEYS = -1e30


def _sb_kernel(q_ref, k_ref, v_ref, o_ref):
    t, nsub = SB_TILE, SB_SUB
    first = nsub * pl.program_id(1)
    left = lax.broadcasted_iota(jnp.int32, (1, LANES), 1) < SB_HEAD_DIM
    q_heads = []
    for s in range(nsub):
        q = q_ref[s * t:(s + 1) * t, :]
        qz = jnp.zeros_like(q)
        q_heads.append((jnp.where(left, q, qz), jnp.where(left, qz, q)))
    row = lax.broadcasted_iota(jnp.int32, (t, t), 0)
    col = lax.broadcasted_iota(jnp.int32, (t, t), 1)
    neg_tri = jnp.where(row >= col, -1.0, 0.0).astype(BF16)
    neg_tri2 = jnp.concatenate([neg_tri, neg_tri], axis=0)
    causal = col < row
    chains = [(s, hh) for s in range(nsub) for hh in range(2)]

    def step(back, diagonal, accs, cs):
        ks, vs = [], []
        for s in range(nsub):
            kb = first + s - back
            if not diagonal:
                cs = [jnp.where(kb >= 0, c, SB_NO_KEYS) if ch[0] == s else c for c, ch in zip(cs, chains)]
                kb = jnp.maximum(kb, 0)
            start = pl.multiple_of(kb * t, t)
            ks.append(k_ref[pl.ds(start, t), :])
            v = v_ref[pl.ds(start, t), :]
            vz = jnp.zeros_like(v)
            vs.append((jnp.where(left, v, vz), jnp.where(left, vz, v)))
        z = [lax.dot_general(q_heads[s][hh], ks[s], _NT, preferred_element_type=F32) for s, hh in chains]
        sp = [jnp.maximum(jnp.log(1.0 + jnp.exp(jnp.minimum(x, SB_EXP_CLAMP))), x) for x in z]
        if diagonal:
            sp = [jnp.where(causal, x, 0.0) for x in sp]
        cum = [jnp.dot(jnp.concatenate(_split(x, 2), axis=1), neg_tri2, preferred_element_type=F32) for x in sp]
        w = [jnp.exp(zz + cc + c) for zz, cc, c in zip(z, cum, cs)]
        if diagonal:
            w = [jnp.where(causal, x, 0.0) for x in w]
        pv = [jnp.dot(x.astype(BF16), vs[s][hh], preferred_element_type=F32) for x, (s, hh) in zip(w, chains)]
        accs = [accs[s] + pv[2 * s] + pv[2 * s + 1] for s in range(nsub)]
        cs = [c + cc[:, :1] for c, cc in zip(cs, cum)]
        live = jnp.max(functools.reduce(jnp.maximum, cs)) >= SB_LOG_ZERO
        return accs, cs, live.astype(jnp.int32)

    accs = [jnp.zeros((t, LANES), F32) for _ in range(nsub)]
    cs = [jnp.zeros((t, 1), F32) for _ in chains]
    accs, cs, live = step(0, True, accs, cs)

    def more(carry):
        return jnp.logical_and(carry[0] < first + nsub, carry[1] > 0)

    def walk(carry):
        accs, cs, live = step(carry[0], False, list(carry[2]), list(carry[3]))
        return carry[0] + 1, live, tuple(accs), tuple(cs)

    carry = lax.while_loop(more, walk, (jnp.int32(1), live, tuple(accs), tuple(cs)))
    for s in range(nsub):
        o_ref[s * t:(s + 1) * t, :] = carry[2][s].astype(o_ref.dtype)


def _sb_attention(qkv):
    s = qkv.shape[0]
    t = SB_TILE
    tq = SB_SUB * t
    npair = BRANCH // LANES
    nbytes = 2 * 2 * s * LANES * 2 + 16 * SB_SUB * t * t * 4
    return pl.pallas_call(
        _sb_kernel,
        grid=(npair, s // tq),
        in_specs=[
            pl.BlockSpec((tq, LANES), lambda p, i: (i, p)),
            pl.BlockSpec((s, LANES), lambda p, i: (0, npair + p)),
            pl.BlockSpec((s, LANES), lambda p, i: (0, 2 * npair + p)),
        ],
        out_specs=pl.BlockSpec((tq, LANES), lambda p, i: (i, p)),
        out_shape=jax.ShapeDtypeStruct((s, BRANCH), BF16),
        compiler_params=_params(("parallel", "arbitrary"), nbytes),
        name="sb_attention",
    )(qkv, qkv, qkv)


GLA_TILE = 256


def _gla_kernel(q_ref, k_ref, v_ref, r_ref, al_ref, gup_ref, gb_ref, gn_ref, o_ref, st_ref):
    t, c, dk, dv = GLA_TILE, GLA_CHUNK, GLA_DK, GLA_DV

    @pl.when(pl.program_id(0) == 0)
    def _():
        st_ref[...] = jnp.zeros_like(st_ref)

    x = jnp.dot(al_ref[...].astype(BF16), gup_ref[...], preferred_element_type=F32) + gb_ref[...]
    g = -_softplus(-x) * (1.0 / GLA_TAU)
    row = lax.broadcasted_iota(jnp.int32, (t, t), 0)
    col = lax.broadcasted_iota(jnp.int32, (t, t), 1)
    same = (row // c) == (col // c)
    tril = same & (col <= row)
    g_parts = _split(g, 3)
    b_all = _mm([jnp.where(tril, 1.0, 0.0).astype(BF16)], g_parts)
    last_all = _mm([jnp.where(same, 1.0, 0.0).astype(BF16)], g_parts)
    hs = range(GLA_HEADS)
    kcols = [slice(h * dk, (h + 1) * dk) for h in hs]
    vcols = [slice(h * dv, (h + 1) * dv) for h in hs]
    b = [b_all[:, sl] for sl in kcols]
    b_last = [last_all[:, sl] for sl in kcols]
    k = [k_ref[:, sl] for sl in kcols]
    v = [v_ref[:, sl].astype(BF16) for sl in vcols]
    q_dec = [(q_ref[:, sl] * (dk ** -0.5) * jnp.exp(bb)).astype(BF16) for sl, bb in zip(kcols, b)]
    k_dec = [(kk * jnp.exp(-bb)).astype(BF16) for kk, bb in zip(k, b)]
    k_tail = [(kk * jnp.exp(bl - bb)).astype(BF16) for kk, bl, bb in zip(k, b_last, b)]
    chunk_decay = [jnp.exp(bl) for bl in b_last]
    scores = [lax.dot_general(qd, kd, _NT, preferred_element_type=F32) for qd, kd in zip(q_dec, k_dec)]
    scores = [jnp.where(tril, sc, 0.0).astype(BF16) for sc in scores]
    o_intra = [jnp.dot(sc, vv, preferred_element_type=F32) for sc, vv in zip(scores, v)]
    v_t = [vv.astype(F32).T.astype(BF16) for vv in v]
    st = [st_ref[h] for h in hs]
    outs = [[] for _ in hs]
    for n in range(t // c):
        rows = slice(n * c, (n + 1) * c)
        o_inter = [lax.dot_general(qd[rows], s.astype(BF16), _NT, preferred_element_type=F32) for qd, s in zip(q_dec, st)]
        for h in hs:
            outs[h].append(o_intra[h][rows] + o_inter[h])
        kv = [jnp.dot(vt[:, rows], kt[rows], preferred_element_type=F32) for vt, kt in zip(v_t, k_tail)]
        st = [s * cd[n * c:n * c + 1, :] + x for s, cd, x in zip(st, chunk_decay, kv)]
    for h in hs:
        st_ref[h] = st[h]
        o = jnp.concatenate(outs[h], axis=0)
        o = o * lax.rsqrt(jnp.mean(o * o, axis=-1, keepdims=True) + NORM_EPS)
        r = r_ref[:, vcols[h]]
        o_ref[:, vcols[h]] = (o * gn_ref[:, vcols[h]] * (r * jax.nn.sigmoid(r))).astype(o_ref.dtype)


def _gla(y, gate_up, gate_bias, gla_norm):
    s = y.shape[0]
    t = min(GLA_TILE, s)
    kw, vw = GLA_HEADS * GLA_DK, GLA_HEADS * GLA_DV
    nbytes = 2 * t * (2 * kw + 3 * vw + LANES) * 4 + GLA_HEADS * 12 * t * t * 4
    return pl.pallas_call(
        _gla_kernel,
        grid=(s // t,),
        in_specs=[
            pl.BlockSpec((t, kw), lambda i: (i, C_GLA_Q // kw)),
            pl.BlockSpec((t, kw), lambda i: (i, C_GLA_K // kw)),
            pl.BlockSpec((t, vw), lambda i: (i, C_GLA_V // vw)),
            pl.BlockSpec((t, vw), lambda i: (i, C_GLA_R // vw)),
            pl.BlockSpec((t, LANES), lambda i: (i, C_ALPHA // LANES)),
            pl.BlockSpec((LANES, kw), lambda i: (0, 0)),
            pl.BlockSpec((1, kw), lambda i: (0, 0)),
            pl.BlockSpec((1, vw), lambda i: (0, 0)),
        ],
        out_specs=pl.BlockSpec((t, vw), lambda i: (i, 0)),
        out_shape=jax.ShapeDtypeStruct((s, BRANCH), BF16),
        scratch_shapes=[pltpu.VMEM((GLA_HEADS, GLA_DV, GLA_DK), F32)],
        compiler_params=_params(("arbitrary",), nbytes),
        name="gla",
    )(y, y, y, y, y, gate_up, gate_bias, gla_norm)


def _head_ones():
    r = lax.broadcasted_iota(jnp.int32, (MXU_DIM, MXU_DIM), 0) // RW_N
    c = lax.broadcasted_iota(jnp.int32, (MXU_DIM, MXU_DIM), 1) // RW_N
    return jnp.where(r == c, 1.0, 0.0).astype(BF16)


def _head_sum(x, ones_bd, n_split):
    outs = []
    for i in range(x.shape[1] // MXU_DIM):
        outs.append(_mm(_split(x[:, i * MXU_DIM:(i + 1) * MXU_DIM], n_split), [ones_bd], order=n_split))
    return jnp.concatenate(outs, axis=1)


def _rwkv_prep_kernel(yr_ref, yk_ref, yv_ref, ywa_ref, yg_ref, pr_ref, pk_ref, pv_ref, pwa_ref, pg_ref,
                      mur_ref, muk_ref, muv_ref, muwa_ref, mug_ref, wup_ref, aup_ref, gup_ref,
                      w0_ref, a0_ref, kk_ref, ka_ref, rk_ref,
                      r_out, lw_out, k_out, v_out, kk_out, a_out, g_out, bonus_out):
    first = pl.program_id(0) == 0

    def shifted(y_ref, p_ref, mu_ref):
        y = y_ref[...]
        prev_row = jnp.where(first, 0.0, p_ref[7:8, :])
        row = lax.broadcasted_iota(jnp.int32, y.shape, 0)
        y_prev = jnp.where(row == 0, prev_row, pltpu.roll(y, 1, axis=0))
        return y + (y_prev - y) * mu_ref[...]

    r = shifted(yr_ref, pr_ref, mur_ref)
    k = shifted(yk_ref, pk_ref, muk_ref)
    v = shifted(yv_ref, pv_ref, muv_ref)
    wa = shifted(ywa_ref, pwa_ref, muwa_ref)
    gl = shifted(yg_ref, pg_ref, mug_ref)
    w_pre = w0_ref[...] + jnp.dot(jnp.tanh(wa).astype(BF16), wup_ref[...], preferred_element_type=F32)
    lw = -jnp.exp(-_softplus(-w_pre) - 0.5)
    a = jax.nn.sigmoid(a0_ref[...] + jnp.dot(wa.astype(BF16), aup_ref[...], preferred_element_type=F32))
    g = jnp.dot(jax.nn.sigmoid(gl).astype(BF16), gup_ref[...], preferred_element_type=F32)
    ones_bd = _head_ones()
    kk = k * kk_ref[...]
    kk = kk / jnp.maximum(jnp.sqrt(_head_sum(kk * kk, ones_bd, 3)), 1e-12)
    k = k * (1.0 + (a - 1.0) * ka_ref[...])
    bonus = _head_sum(r * k * rk_ref[...], ones_bd, 3) * v
    r_out[...] = r
    lw_out[...] = lw
    k_out[...] = k
    v_out[...] = v
    kk_out[...] = kk
    a_out[...] = a
    g_out[...] = g
    bonus_out[...] = bonus


def _rwkv_prep(y, mu_r, mu_k, mu_v, mu_wa, mu_g, wup, aup, gup, w0, a0, k_k, k_a, r_k, tm=512):
    s = y.shape[0]
    tm = min(tm, s)
    w = BRANCH

    def cur(width, col):
        return pl.BlockSpec((tm, width), lambda i: (i, col // width))

    def prev(width, col):
        return pl.BlockSpec((8, width), lambda i: (jnp.maximum(i * (tm // 8) - 1, 0), col // width))

    def full(shape):
        return pl.BlockSpec(shape, lambda i: (0, 0))

    nbytes = 2 * tm * (3 * w + 384) * 4 + 2 * 8 * tm * w * 4 + 12 * tm * w * 4
    out = jax.ShapeDtypeStruct((s, w), F32)
    return pl.pallas_call(
        _rwkv_prep_kernel,
        grid=(s // tm,),
        in_specs=[cur(w, C_RW_R), cur(w, C_RW_K), cur(w, C_RW_V), cur(LANES, C_WA), cur(2 * LANES, C_GLOW),
                  prev(w, C_RW_R), prev(w, C_RW_K), prev(w, C_RW_V), prev(LANES, C_WA), prev(2 * LANES, C_GLOW),
                  full((1, w)), full((1, w)), full((1, w)), full((1, LANES)), full((1, 2 * LANES)),
                  full((LANES, w)), full((LANES, w)), full((2 * LANES, w)),
                  full((1, w)), full((1, w)), full((1, w)), full((1, w)), full((1, w))],
        out_specs=[pl.BlockSpec((tm, w), lambda i: (i, 0))] * 8,
        out_shape=[out] * 8,
        compiler_params=_params(("parallel",), nbytes),
        name="rwkv_prep",
    )(y, y, y, y, y, y, y, y, y, y, mu_r, mu_k, mu_v, mu_wa, mu_g, wup, aup, gup, w0, a0, k_k, k_a, r_k)


RW_CHUNK = 64
RW_PACK = MXU_DIM // RW_N
RW_NEUMANN = 8
RW_STEP_CHUNKS = 4


def _rwkv_kernel(r_ref, lw_ref, k_ref, v_ref, kk_ref, a_ref, o_ref, s_ref):
    c, g4 = RW_CHUNK, MXU_DIM

    @pl.when(pl.program_id(0) == 0)
    def _():
        s_ref[...] = jnp.zeros_like(s_ref)

    ti = lax.broadcasted_iota(jnp.int32, (c, g4), 0)
    ci = lax.broadcasted_iota(jnp.int32, (c, g4), 1) % c
    strict = ci < ti
    incl = ci <= ti
    lane_head = lax.broadcasted_iota(jnp.int32, (1, g4), 1) // RW_N
    head_masks = [lane_head == h for h in range(RW_PACK)]
    bd_mask = (lax.broadcasted_iota(jnp.int32, (g4, g4), 0) // RW_N) == (lax.broadcasted_iota(jnp.int32, (g4, g4), 1) // RW_N)
    tr = lax.broadcasted_iota(jnp.int32, (RW_STEP_CHUNKS * c, RW_STEP_CHUNKS * c), 0)
    tc = lax.broadcasted_iota(jnp.int32, (RW_STEP_CHUNKS * c, RW_STEP_CHUNKS * c), 1)
    tri = jnp.where((tc <= tr) & (tc // c == tr // c), 1.0, 0.0).astype(BF16)

    def heads(b):
        p = b.astype(BF16)
        return jnp.concatenate([jnp.where(m, p, jnp.zeros_like(p)) for m in head_masks], axis=0)

    groups = range(BRANCH // g4)
    lanes = [slice(grp * g4, (grp + 1) * g4) for grp in groups]
    rows = [slice(n * c, (n + 1) * c) for n in range(RW_STEP_CHUNKS)]
    units = [(rw, sl) for rw in rows for sl in lanes]

    def each(fn, *cols):
        return [fn(*args) for args in zip(*cols)]

    def prod(a, b, dims=_NN):
        m = a.shape[0]
        both = lax.dot_general(jnp.concatenate(_split(a, 2), axis=0), b, dims, preferred_element_type=F32)
        return both[:m] + both[m:]

    lw_all = lw_ref[...]
    cum_all = _mm([tri], _split(lw_all, 3))
    lw = [lw_all[rw, sl] for rw, sl in units]
    cum = [cum_all[rw, sl] for rw, sl in units]
    cum_end = [x[c - 1:c, :] for x in cum]
    kk = [kk_ref[rw, sl] for rw, sl in units]
    kka = [x * a_ref[rw, sl] for x, (rw, sl) in zip(kk, units)]
    k = [k_ref[rw, sl] for rw, sl in units]
    v = [v_ref[rw, sl] for rw, sl in units]
    g_inv = each(lambda x: jnp.exp(-x), cum)
    tail = each(lambda e, x: jnp.exp(e - x), cum_end, cum)
    alp = each(lambda x, cm, l: x * jnp.exp(cm - l), kk, cum, lw)
    rho = each(lambda un, cm: r_ref[un[0], un[1]] * jnp.exp(cm), units, cum)
    ar = each(lambda x, y: jnp.concatenate([x, y], axis=0), alp, rho)
    pb = each(lambda x, y, gi: prod(x, heads(y * gi), _NT), ar, kka, g_inv)
    pk = each(lambda x, y, gi: prod(x, heads(y * gi), _NT), ar, k, g_inv)
    l_b = each(lambda p: jnp.where(strict, p[:c], 0.0), pb)
    m_b = each(lambda p: jnp.where(incl, p[c:], 0.0), pb)
    lm_k = each(lambda p: jnp.concatenate([jnp.where(strict, p[:c], 0.0), jnp.where(incl, p[c:], 0.0)], axis=0), pk)

    nb = RW_NEUMANN
    eye = jnp.where(ci == ti, 1.0, 0.0)
    l1 = each(lambda ll: jnp.where((ci // nb) == (ti // nb), ll, 0.0), l_b)
    l2 = each(lambda ll: prod(ll, heads(ll)), l1)
    l34 = each(lambda ll, sq: prod(jnp.concatenate([ll, sq], axis=0), heads(sq)), l1, l2)
    a1 = each(lambda ll, sq, p: eye - ll + sq - p[:c], l1, l2, l34)
    x = each(lambda aa, p: aa + prod(aa, heads(p[c:])), a1, l34)
    m = nb
    while m < c:
        lower = ((ti // m) % 2 == 1) & ((ci // m) == (ti // m) - 1)
        y = each(lambda xx, ll: prod(xx, heads(jnp.where(lower, ll, 0.0))), x, l_b)
        x = each(lambda xx, yy: xx - prod(yy, heads(xx)), x, y)
        m *= 2

    lmv = each(lambda lm, vv: prod(lm, heads(vv)), lm_k, v)
    kb = each(lambda kx, ka, t: jnp.concatenate([kx * t, ka * t], axis=0).astype(BF16), k, kka, tail)
    g_end = each(jnp.exp, cum_end)

    st = [s_ref[grp] for grp in groups]
    ng = len(lanes)
    for n, rw in enumerate(rows):
        mine = slice(n * ng, (n + 1) * ng)
        ars = each(lambda a, s: prod(a, s.astype(BF16), _NT), ar[mine], st)
        u = each(lambda xx, a, p: prod(xx, heads(a[:c] + p[:c])), x[mine], ars, lmv[mine])
        o = each(lambda a, p, mb, uu: a[c:] + p[c:] - prod(mb, heads(uu)), ars, lmv[mine], m_b[mine], u)
        vu_t = each(lambda vv, uu: jnp.concatenate([vv, -uu], axis=0).T, v[mine], u)
        st = each(lambda s, e, a, b: jnp.where(bd_mask, s * e + prod(a, b), 0.0), st, g_end[mine], vu_t, kb[mine])
        for grp in groups:
            o_ref[rw, lanes[grp]] = o[grp]
    for grp in groups:
        s_ref[grp] = st[grp]


def _rwkv_recurrence(r, lw, k, v, kk, a):
    s, w = r.shape
    c = RW_STEP_CHUNKS * RW_CHUNK
    spec = pl.BlockSpec((c, w), lambda i: (i, 0))
    nbytes = 2 * 7 * c * w * 4 + (w // MXU_DIM) * MXU_DIM * MXU_DIM * 4 + RW_STEP_CHUNKS * 64 * MXU_DIM * MXU_DIM * 4
    return pl.pallas_call(
        _rwkv_kernel,
        grid=(s // c,),
        in_specs=[spec] * 6,
        out_specs=spec,
        out_shape=jax.ShapeDtypeStruct((s, w), F32),
        scratch_shapes=[pltpu.VMEM((w // MXU_DIM, MXU_DIM, MXU_DIM), F32)],
        compiler_params=_params(("arbitrary",), nbytes),
        name="rwkv_recurrence",
    )(r, lw, k, v, kk, a)


def _rwkv_post_kernel(wkv_ref, g_ref, bonus_ref, lnw_ref, lnb_ref, o_ref):
    ones_bd = _head_ones()
    x = wkv_ref[...]
    d = x - _head_sum(x, ones_bd, 3) * (1.0 / RW_N)
    var = _head_sum(d * d, ones_bd, 3) * (1.0 / RW_N)
    y = d * lax.rsqrt(var + RW_LN_EPS) * lnw_ref[...] + lnb_ref[...]
    o_ref[...] = ((y + bonus_ref[...]) * g_ref[...]).astype(o_ref.dtype)


def _rwkv_post(wkv, g, bonus, ln_w, ln_b, tm=512):
    s, w = wkv.shape
    tm = min(tm, s)
    blk = pl.BlockSpec((tm, w), lambda i: (i, 0))
    vec = pl.BlockSpec((1, w), lambda i: (0, 0))
    nbytes = 2 * 4 * tm * w * 4 + 8 * tm * w * 4
    return pl.pallas_call(
        _rwkv_post_kernel,
        grid=(s // tm,),
        in_specs=[blk, blk, blk, vec, vec],
        out_specs=blk,
        out_shape=jax.ShapeDtypeStruct((s, w), BF16),
        compiler_params=_params(("parallel",), nbytes),
        name="rwkv_post",
    )(wkv, g, bonus, ln_w, ln_b)


def _merge_kernel(oa_ref, ob_ref, oc_ref, wb_ref, ga_ref, gb_ref, gc_ref, o_ref):
    acc = None
    for o_g, gate, idx in ((oa_ref, ga_ref, 0), (ob_ref, gb_ref, 1), (oc_ref, gc_ref, 2)):
        u = jnp.dot(o_g[...], wb_ref[idx], preferred_element_type=F32)
        t = jax.nn.sigmoid(gate[...]) * u
        acc = t if acc is None else acc + t
    o_ref[...] = acc.astype(o_ref.dtype)


def _merge(oa, ob, oc, w_branch, layer, y, tm=1024, tn=512):
    s, w = oa.shape
    d = w_branch.shape[3]
    tm = min(tm, s)
    branch = pl.BlockSpec((tm, w), lambda i, j: (i, 0))

    def gate(gi):
        return pl.BlockSpec((tm, tn), lambda i, j: (i, (C_GATES + gi * d) // tn + j))

    nbytes = 2 * 3 * tm * w * 2 + 2 * 3 * w * tn * 2 + 2 * 4 * tm * tn * 4 + 4 * tm * tn * 4
    return pl.pallas_call(
        _merge_kernel,
        grid=(s // tm, d // tn),
        in_specs=[branch, branch, branch, pl.BlockSpec((None, 3, w, tn), lambda i, j: (layer, 0, 0, j)),
                  gate(0), gate(1), gate(2)],
        out_specs=pl.BlockSpec((tm, tn), lambda i, j: (i, j)),
        out_shape=jax.ShapeDtypeStruct((s, d), BF16),
        compiler_params=_params(("parallel", "arbitrary"), nbytes),
        name="merge",
    )(oa, ob, oc, w_branch, y, y, y)


def _out_proj_kernel(m_ref, w_ref, x_ref, g_ref, o_ref):
    mix = jnp.dot(m_ref[...], w_ref[...], preferred_element_type=F32)
    o_ref[...] = x_ref[...] + _rms(mix, g_ref[...])


def _out_proj(merged, w_out, layer, x, gain, tm=512):
    s, d = x.shape
    tm = min(tm, s)
    nbytes = 2 * tm * d * 2 + 2 * d * d * 2 + 4 * tm * d * 4 + 2 * tm * d * 4
    return pl.pallas_call(
        _out_proj_kernel,
        grid=(s // tm,),
        in_specs=[pl.BlockSpec((tm, d), lambda i: (i, 0)), pl.BlockSpec((None, d, d), lambda i: (layer, 0, 0)),
                  pl.BlockSpec((tm, d), lambda i: (i, 0)), pl.BlockSpec((1, d), lambda i: (0, 0))],
        out_specs=pl.BlockSpec((tm, d), lambda i: (i, 0)),
        out_shape=jax.ShapeDtypeStruct((s, d), F32),
        compiler_params=_params(("parallel",), nbytes),
        name="out_proj",
    )(merged, w_out, x, gain)


def _pad_cols(w, width):
    return jnp.pad(w, [(0, 0)] * (w.ndim - 1) + [(0, width - w.shape[-1])])


def _pad_rows(w, height, offset=0):
    return jnp.pad(w, ((offset, height - offset - w.shape[0]), (0, 0)))


def _pack_w_in(w):
    b = BRANCH
    gla0 = 3 * b
    alpha0 = gla0 + 2 * GLA_HEADS * GLA_DK + 2 * b
    rw0 = alpha0 + GLA_RANK
    wa0 = rw0 + 3 * b
    gl0 = wa0 + 2 * RW_RANK
    gates0 = gl0 + RW_GATE_RANK
    wt = jnp.swapaxes(w, 1, 2)

    def pad_to(t, rows):
        return jnp.pad(t, ((0, 0), (0, rows - t.shape[1]), (0, 0)))

    return jnp.concatenate([
        wt[:, :b] * (SB_HEAD_DIM ** -0.5), wt[:, b:3 * b],
        wt[:, gla0:alpha0], wt[:, rw0:wa0], wt[:, gates0:],
        pad_to(wt[:, alpha0:rw0], LANES), wt[:, wa0:gl0], pad_to(wt[:, gl0:gates0], 2 * LANES)],
        axis=1).astype(BF16)


def _token_mixing(x, gain, w_packed, layer, gla_gate_up, gla_gate_bias, gla_norm, rwkv_mu, rwkv_w_up, rwkv_w0,
                  rwkv_a_up, rwkv_a0, rwkv_g_up, rwkv_k_k, rwkv_k_a, rwkv_r_k, rwkv_ln_w, rwkv_ln_b,
                  w_branch, w_out, gain_post):
    b = BRANCH
    qkv = _norm_matmul(x, gain, w_packed, layer, 0, N_SB, BF16)
    y = _norm_matmul(x, gain, w_packed, layer, N_SB, N_REST, F32)

    o_a = _sb_attention(qkv)

    o_b = _gla(y, _pad_rows(gla_gate_up, LANES).astype(BF16), gla_gate_bias[None, :], gla_norm[None, :])

    mu = rwkv_mu
    row = lambda t: t[None, :]
    prep = _rwkv_prep(
        y, row(mu[:b]), row(mu[b:2 * b]), row(mu[2 * b:3 * b]), row(mu[3 * b:3 * b + 2 * RW_RANK]),
        _pad_cols(row(mu[3 * b + 2 * RW_RANK:]), 2 * LANES),
        _pad_rows(rwkv_w_up, LANES).astype(BF16), _pad_rows(rwkv_a_up, LANES, RW_RANK).astype(BF16),
        _pad_rows(rwkv_g_up, 2 * LANES).astype(BF16),
        row(rwkv_w0), row(rwkv_a0), row(rwkv_k_k), row(rwkv_k_a), row(rwkv_r_k.reshape(-1)))
    r, lw, k, v, kk, a, g, bonus = prep
    wkv = _rwkv_recurrence(r, lw, k, v, kk, a)
    o_c = _rwkv_post(wkv, g, bonus, row(rwkv_ln_w), row(rwkv_ln_b))

    merged = _merge(o_a, o_b, o_c, w_branch, layer, y)
    return _out_proj(merged, w_out, layer, x, gain_post)


def kernel(x, norm_pre, norm_post, ffn_in, ffn_out, w_in, gla_gate_up, gla_gate_bias, gla_norm, rwkv_mu,
           rwkv_w_up, rwkv_w0, rwkv_a_up, rwkv_a0, rwkv_g_up, rwkv_k_k, rwkv_k_a, rwkv_r_k, rwkv_ln_w,
           rwkv_ln_b, w_branch, w_out):
    batch, seq, d = x.shape
    assert batch == 1 and d == D_MODEL
    h = x[0]
    w_packed = _pack_w_in(w_in)
    w_branch, w_out = w_branch.astype(BF16), w_out.astype(BF16)
    for l in range(norm_pre.shape[0]):
        h = _ffn(h, norm_pre[l, 0][None, :], norm_post[l, 0][None, :], ffn_in, ffn_out, l, 0)
        h = _token_mixing(h, norm_pre[l, 1][None, :], w_packed, l, gla_gate_up[l], gla_gate_bias[l], gla_norm[l],
                          rwkv_mu[l], rwkv_w_up[l], rwkv_w0[l], rwkv_a_up[l], rwkv_a0[l], rwkv_g_up[l],
                          rwkv_k_k[l], rwkv_k_a[l], rwkv_r_k[l], rwkv_ln_w[l], rwkv_ln_b[l],
                          w_branch, w_out, norm_post[l, 1][None, :])
        h = _ffn(h, norm_pre[l, 2][None, :], norm_post[l, 2][None, :], ffn_in, ffn_out, l, 1)
    return h[None]
```

```python
import functools

import jax
import jax.numpy as jnp
from jax import lax
from jax.experimental import pallas as pl
from jax.experimental.pallas import tpu as pltpu

F32 = jnp.float32
BF16 = jnp.bfloat16

D_MODEL = 2048
BRANCH = D_MODEL // 2
FFN_SCALE = 0.5
NORM_EPS = 1e-6

SB_HEAD_DIM = 64
GLA_HEADS = 4
GLA_DV = BRANCH // GLA_HEADS
GLA_DK = GLA_DV // 2
GLA_RANK = 16
GLA_TAU = 16.0
GLA_CHUNK = 64
RW_N = 64
RW_RANK = 64
RW_GATE_RANK = 160
RW_LN_EPS = 64e-5

LANES = 128
MXU_DIM = 256
VMEM_LIMIT_CAP = 60000 * 1024

C_GLA_Q = 0
C_GLA_K = 512
C_GLA_V = 1024
C_GLA_R = 2048
C_RW_R = 3072
C_RW_K = 4096
C_RW_V = 5120
C_GATES = 6144
C_ALPHA = 12288
C_WA = 12416
C_GLOW = 12544
N_REST = 12800
N_SB = 3 * BRANCH


def _vmem_limit(nbytes):
    return int(min(max(2 * nbytes, 32 * 1024 * 1024), VMEM_LIMIT_CAP))


def _params(sem, nbytes):
    return pltpu.CompilerParams(dimension_semantics=sem, vmem_limit_bytes=_vmem_limit(nbytes))


def _rms(x, gain):
    ms = jnp.mean(x * x, axis=-1, keepdims=True)
    return x * lax.rsqrt(ms + NORM_EPS) * gain


def _softplus(x):
    return jnp.maximum(x, 0.0) + jnp.log(1.0 + jnp.exp(-jnp.abs(x)))


def _split(x, n):
    parts = []
    for _ in range(n - 1):
        p = x.astype(BF16)
        parts.append(p)
        x = x - p.astype(F32)
    parts.append(x.astype(BF16))
    return parts


_NN = (((1,), (0,)), ((), ()))
_NT = (((1,), (1,)), ((), ()))


def _mm(a_parts, b_parts, dims=_NN, order=3):
    acc = None
    for i, a in enumerate(a_parts):
        for j, b in enumerate(b_parts):
            if i + j < order:
                t = lax.dot_general(a, b, dims, preferred_element_type=F32)
                acc = t if acc is None else acc + t
    return acc


def _ffn_kernel(x_ref, gpre_ref, gpost_ref, wg_ref, wu_ref, wo_ref, o_ref, h_ref, acc_ref):
    j = pl.program_id(1)

    @pl.when(j == 0)
    def _():
        h_ref[...] = _rms(x_ref[...], gpre_ref[...]).astype(BF16)
        acc_ref[...] = jnp.zeros_like(acc_ref)

    h = h_ref[...]
    g = jnp.dot(h, wg_ref[...].astype(BF16), preferred_element_type=F32)
    u = jnp.dot(h, wu_ref[...].astype(BF16), preferred_element_type=F32)
    a = (g * jax.nn.sigmoid(g) * u).astype(BF16)
    acc_ref[...] += jnp.dot(a, wo_ref[...].astype(BF16), preferred_element_type=F32)

    @pl.when(j == pl.num_programs(1) - 1)
    def _():
        o_ref[...] = x_ref[...] + FFN_SCALE * _rms(acc_ref[...], gpost_ref[...])


def _ffn(x, gpre, gpost, w_in, w_out, layer, which, tm=1024, tf=256):
    s, d = x.shape
    f = w_out.shape[2]
    tm = min(tm, s)
    nj = f // tf
    nbytes = (2 * tm * d * 4) + tm * d * 6 + 2 * (3 * d * tf * 4) + 3 * d * tf * 2 + 4 * tm * tf * 4
    return pl.pallas_call(
        _ffn_kernel,
        grid=(s // tm, nj),
        in_specs=[
            pl.BlockSpec((tm, d), lambda i, j: (i, 0)),
            pl.BlockSpec((1, d), lambda i, j: (0, 0)),
            pl.BlockSpec((1, d), lambda i, j: (0, 0)),
            pl.BlockSpec((None, None, d, tf), lambda i, j: (layer, which, 0, j)),
            pl.BlockSpec((None, None, d, tf), lambda i, j: (layer, which, 0, j + nj)),
            pl.BlockSpec((None, None, tf, d), lambda i, j: (layer, which, j, 0)),
        ],
        out_specs=pl.BlockSpec((tm, d), lambda i, j: (i, 0), pipeline_mode=pl.Buffered(1)),
        out_shape=jax.ShapeDtypeStruct((s, d), F32),
        scratch_shapes=[pltpu.VMEM((tm, d), BF16), pltpu.VMEM((tm, d), F32)],
        compiler_params=_params(("parallel", "arbitrary"), nbytes),
        name="ffn",
    )(x, gpre, gpost, w_in, w_in, w_out)


def _norm_matmul_kernel(x_ref, g_ref, w_ref, o_ref, h_ref):
    @pl.when(pl.program_id(1) == 0)
    def _():
        h_ref[...] = _rms(x_ref[...], g_ref[...]).astype(BF16)

    o_ref[...] = lax.dot_general(h_ref[...], w_ref[...], _NT, preferred_element_type=F32).astype(o_ref.dtype)


def _norm_matmul(x, gain, w, layer, col0, n, out_dtype, tm=2048, tn=512):
    s, d = x.shape
    tm = min(tm, s)
    nbytes = 2 * tm * d * 4 + tm * d * 2 + 2 * d * tn * 2 + 2 * tm * tn * 4
    return pl.pallas_call(
        _norm_matmul_kernel,
        grid=(s // tm, n // tn),
        in_specs=[
            pl.BlockSpec((tm, d), lambda i, j: (i, 0)),
            pl.BlockSpec((1, d), lambda i, j: (0, 0)),
            pl.BlockSpec((None, tn, d), lambda i, j: (layer, col0 // tn + j, 0)),
        ],
        out_specs=pl.BlockSpec((tm, tn), lambda i, j: (i, j)),
        out_shape=jax.ShapeDtypeStruct((s, n), out_dtype),
        scratch_shapes=[pltpu.VMEM((tm, d), BF16)],
        compiler_params=_params(("parallel", "arbitrary"), nbytes),
        name="norm_matmul",
    )(x, gain, w)


SB_TILE = 256
SB_SUB = 4
SB_EXP_CLAMP = 40.0
SB_LOG_ZERO = -110.0
SB_NO_KEYS = -1e30


def _sb_kernel(q_ref, k_ref, v_ref, o_ref):
    t, nsub = SB_TILE, SB_SUB
    first = nsub * pl.program_id(1)
    left = lax.broadcasted_iota(jnp.int32, (1, LANES), 1) < SB_HEAD_DIM
    q_heads = []
    for s in range(nsub):
        q = q_ref[s * t:(s + 1) * t, :]
        qz = jnp.zeros_like(q)
        q_heads.append((jnp.where(left, q, qz), jnp.where(left, qz, q)))
    row = lax.broadcasted_iota(jnp.int32, (t, t), 0)
    col = lax.broadcasted_iota(jnp.int32, (t, t), 1)
    neg_tri = jnp.where(row >= col, -1.0, 0.0).astype(BF16)
    neg_tri2 = jnp.concatenate([neg_tri, neg_tri], axis=0)
    causal = col < row
    chains = [(s, hh) for s in range(nsub) for hh in range(2)]

    def step(back, diagonal, accs, cs):
        ks, vs = [], []
        for s in range(nsub):
            kb = first + s - back
            if not diagonal:
                cs = [jnp.where(kb >= 0, c, SB_NO_KEYS) if ch[0] == s else c for c, ch in zip(cs, chains)]
                kb = jnp.maximum(kb, 0)
            start = pl.multiple_of(kb * t, t)
            ks.append(k_ref[pl.ds(start, t), :])
            v = v_ref[pl.ds(start, t), :]
            vz = jnp.zeros_like(v)
            vs.append((jnp.where(left, v, vz), jnp.where(left, vz, v)))
        z = [lax.dot_general(q_heads[s][hh], ks[s], _NT, preferred_element_type=F32) for s, hh in chains]
        sp = [jnp.maximum(jnp.log(1.0 + jnp.exp(jnp.minimum(x, SB_EXP_CLAMP))), x) for x in z]
        if diagonal:
            sp = [jnp.where(causal, x, 0.0) for x in sp]
        cum = [jnp.dot(jnp.concatenate(_split(x, 2), axis=1), neg_tri2, preferred_element_type=F32) for x in sp]
        w = [jnp.exp(zz + cc + c) for zz, cc, c in zip(z, cum, cs)]
        if diagonal:
            w = [jnp.where(causal, x, 0.0) for x in w]
        pv = [jnp.dot(x.astype(BF16), vs[s][hh], preferred_element_type=F32) for x, (s, hh) in zip(w, chains)]
        accs = [accs[s] + pv[2 * s] + pv[2 * s + 1] for s in range(nsub)]
        cs = [c + cc[:, :1] for c, cc in zip(cs, cum)]
        live = jnp.max(functools.reduce(jnp.maximum, cs)) >= SB_LOG_ZERO
        return accs, cs, live.astype(jnp.int32)

    accs = [jnp.zeros((t, LANES), F32) for _ in range(nsub)]
    cs = [jnp.zeros((t, 1), F32) for _ in chains]
    accs, cs, live = step(0, True, accs, cs)

    def more(carry):
        return jnp.logical_and(carry[0] < first + nsub, carry[1] > 0)

    def walk(carry):
        accs, cs, live = step(carry[0], False, list(carry[2]), list(carry[3]))
        return carry[0] + 1, live, tuple(accs), tuple(cs)

    carry = lax.while_loop(more, walk, (jnp.int32(1), live, tuple(accs), tuple(cs)))
    for s in range(nsub):
        o_ref[s * t:(s + 1) * t, :] = carry[2][s].astype(o_ref.dtype)


def _sb_attention(qkv):
    s = qkv.shape[0]
    t = SB_TILE
    tq = SB_SUB * t
    npair = BRANCH // LANES
    nbytes = 2 * 2 * s * LANES * 2 + 16 * SB_SUB * t * t * 4
    return pl.pallas_call(
        _sb_kernel,
        grid=(npair, s // tq),
        in_specs=[
            pl.BlockSpec((tq, LANES), lambda p, i: (i, p)),
            pl.BlockSpec((s, LANES), lambda p, i: (0, npair + p)),
            pl.BlockSpec((s, LANES), lambda p, i: (0, 2 * npair + p)),
        ],
        out_specs=pl.BlockSpec((tq, LANES), lambda p, i: (i, p)),
        out_shape=jax.ShapeDtypeStruct((s, BRANCH), BF16),
        compiler_params=_params(("parallel", "arbitrary"), nbytes),
        name="sb_attention",
    )(qkv, qkv, qkv)


GLA_TILE = 256


def _gla_kernel(q_ref, k_ref, v_ref, r_ref, al_ref, gup_ref, gb_ref, gn_ref, o_ref, st_ref):
    t, c, dk, dv = GLA_TILE, GLA_CHUNK, GLA_DK, GLA_DV

    @pl.when(pl.program_id(0) == 0)
    def _():
        st_ref[...] = jnp.zeros_like(st_ref)

    x = jnp.dot(al_ref[...].astype(BF16), gup_ref[...], preferred_element_type=F32) + gb_ref[...]
    g = -_softplus(-x) * (1.0 / GLA_TAU)
    row = lax.broadcasted_iota(jnp.int32, (t, t), 0)
    col = lax.broadcasted_iota(jnp.int32, (t, t), 1)
    same = (row // c) == (col // c)
    tril = same & (col <= row)
    g_parts = _split(g, 3)
    b_all = _mm([jnp.where(tril, 1.0, 0.0).astype(BF16)], g_parts)
    last_all = _mm([jnp.where(same, 1.0, 0.0).astype(BF16)], g_parts)
    hs = range(GLA_HEADS)
    kcols = [slice(h * dk, (h + 1) * dk) for h in hs]
    vcols = [slice(h * dv, (h + 1) * dv) for h in hs]
    b = [b_all[:, sl] for sl in kcols]
    b_last = [last_all[:, sl] for sl in kcols]
    k = [k_ref[:, sl] for sl in kcols]
    v = [v_ref[:, sl].astype(BF16) for sl in vcols]
    q_dec = [(q_ref[:, sl] * (dk ** -0.5) * jnp.exp(bb)).astype(BF16) for sl, bb in zip(kcols, b)]
    k_dec = [(kk * jnp.exp(-bb)).astype(BF16) for kk, bb in zip(k, b)]
    k_tail = [(kk * jnp.exp(bl - bb)).astype(BF16) for kk, bl, bb in zip(k, b_last, b)]
    chunk_decay = [jnp.exp(bl) for bl in b_last]
    scores = [lax.dot_general(qd, kd, _NT, preferred_element_type=F32) for qd, kd in zip(q_dec, k_dec)]
    scores = [jnp.where(tril, sc, 0.0).astype(BF16) for sc in scores]
    o_intra = [jnp.dot(sc, vv, preferred_element_type=F32) for sc, vv in zip(scores, v)]
    v_t = [vv.astype(F32).T.astype(BF16) for vv in v]
    st = [st_ref[h] for h in hs]
    outs = [[] for _ in hs]
    for n in range(t // c):
        rows = slice(n * c, (n + 1) * c)
        o_inter = [lax.dot_general(qd[rows], s.astype(BF16), _NT, preferred_element_type=F32) for qd, s in zip(q_dec, st)]
        for h in hs:
            outs[h].append(o_intra[h][rows] + o_inter[h])
        kv = [jnp.dot(vt[:, rows], kt[rows], preferred_element_type=F32) for vt, kt in zip(v_t, k_tail)]
        st = [s * cd[n * c:n * c + 1, :] + x for s, cd, x in zip(st, chunk_decay, kv)]
    for h in hs:
        st_ref[h] = st[h]
        o = jnp.concatenate(outs[h], axis=0)
        o = o * lax.rsqrt(jnp.mean(o * o, axis=-1, keepdims=True) + NORM_EPS)
        r = r_ref[:, vcols[h]]
        o_ref[:, vcols[h]] = (o * gn_ref[:, vcols[h]] * (r * jax.nn.sigmoid(r))).astype(o_ref.dtype)


def _gla(y, gate_up, gate_bias, gla_norm):
    s = y.shape[0]
    t = min(GLA_TILE, s)
    kw, vw = GLA_HEADS * GLA_DK, GLA_HEADS * GLA_DV
    nbytes = 2 * t * (2 * kw + 3 * vw + LANES) * 4 + GLA_HEADS * 12 * t * t * 4
    return pl.pallas_call(
        _gla_kernel,
        grid=(s // t,),
        in_specs=[
            pl.BlockSpec((t, kw), lambda i: (i, C_GLA_Q // kw)),
            pl.BlockSpec((t, kw), lambda i: (i, C_GLA_K // kw)),
            pl.BlockSpec((t, vw), lambda i: (i, C_GLA_V // vw)),
            pl.BlockSpec((t, vw), lambda i: (i, C_GLA_R // vw)),
            pl.BlockSpec((t, LANES), lambda i: (i, C_ALPHA // LANES)),
            pl.BlockSpec((LANES, kw), lambda i: (0, 0)),
            pl.BlockSpec((1, kw), lambda i: (0, 0)),
            pl.BlockSpec((1, vw), lambda i: (0, 0)),
        ],
        out_specs=pl.BlockSpec((t, vw), lambda i: (i, 0)),
        out_shape=jax.ShapeDtypeStruct((s, BRANCH), BF16),
        scratch_shapes=[pltpu.VMEM((GLA_HEADS, GLA_DV, GLA_DK), F32)],
        compiler_params=_params(("arbitrary",), nbytes),
        name="gla",
    )(y, y, y, y, y, gate_up, gate_bias, gla_norm)


def _head_ones():
    r = lax.broadcasted_iota(jnp.int32, (MXU_DIM, MXU_DIM), 0) // RW_N
    c = lax.broadcasted_iota(jnp.int32, (MXU_DIM, MXU_DIM), 1) // RW_N
    return jnp.where(r == c, 1.0, 0.0).astype(BF16)


def _head_sum(x, ones_bd, n_split):
    outs = []
    for i in range(x.shape[1] // MXU_DIM):
        outs.append(_mm(_split(x[:, i * MXU_DIM:(i + 1) * MXU_DIM], n_split), [ones_bd], order=n_split))
    return jnp.concatenate(outs, axis=1)


def _rwkv_prep_kernel(yr_ref, yk_ref, yv_ref, ywa_ref, yg_ref, pr_ref, pk_ref, pv_ref, pwa_ref, pg_ref,
                      mur_ref, muk_ref, muv_ref, muwa_ref, mug_ref, wup_ref, aup_ref, gup_ref,
                      w0_ref, a0_ref, kk_ref, ka_ref, rk_ref,
                      r_out, lw_out, k_out, v_out, kk_out, a_out, g_out, bonus_out):
    first = pl.program_id(0) == 0

    def shifted(y_ref, p_ref, mu_ref):
        y = y_ref[...]
        prev_row = jnp.where(first, 0.0, p_ref[7:8, :])
        row = lax.broadcasted_iota(jnp.int32, y.shape, 0)
        y_prev = jnp.where(row == 0, prev_row, pltpu.roll(y, 1, axis=0))
        return y + (y_prev - y) * mu_ref[...]

    r = shifted(yr_ref, pr_ref, mur_ref)
    k = shifted(yk_ref, pk_ref, muk_ref)
    v = shifted(yv_ref, pv_ref, muv_ref)
    wa = shifted(ywa_ref, pwa_ref, muwa_ref)
    gl = shifted(yg_ref, pg_ref, mug_ref)
    w_pre = w0_ref[...] + jnp.dot(jnp.tanh(wa).astype(BF16), wup_ref[...], preferred_element_type=F32)
    lw = -jnp.exp(-_softplus(-w_pre) - 0.5)
    a = jax.nn.sigmoid(a0_ref[...] + jnp.dot(wa.astype(BF16), aup_ref[...], preferred_element_type=F32))
    g = jnp.dot(jax.nn.sigmoid(gl).astype(BF16), gup_ref[...], preferred_element_type=F32)
    ones_bd = _head_ones()
    kk = k * kk_ref[...]
    kk = kk / jnp.maximum(jnp.sqrt(_head_sum(kk * kk, ones_bd, 3)), 1e-12)
    k = k * (1.0 + (a - 1.0) * ka_ref[...])
    bonus = _head_sum(r * k * rk_ref[...], ones_bd, 3) * v
    r_out[...] = r
    lw_out[...] = lw
    k_out[...] = k
    v_out[...] = v
    kk_out[...] = kk
    a_out[...] = a
    g_out[...] = g
    bonus_out[...] = bonus


def _rwkv_prep(y, mu_r, mu_k, mu_v, mu_wa, mu_g, wup, aup, gup, w0, a0, k_k, k_a, r_k, tm=512):
    s = y.shape[0]
    tm = min(tm, s)
    w = BRANCH

    def cur(width, col):
        return pl.BlockSpec((tm, width), lambda i: (i, col // width))

    def prev(width, col):
        return pl.BlockSpec((8, width), lambda i: (jnp.maximum(i * (tm // 8) - 1, 0), col // width))

    def full(shape):
        return pl.BlockSpec(shape, lambda i: (0, 0))

    nbytes = 2 * tm * (3 * w + 384) * 4 + 2 * 8 * tm * w * 4 + 12 * tm * w * 4
    out = jax.ShapeDtypeStruct((s, w), F32)
    return pl.pallas_call(
        _rwkv_prep_kernel,
        grid=(s // tm,),
        in_specs=[cur(w, C_RW_R), cur(w, C_RW_K), cur(w, C_RW_V), cur(LANES, C_WA), cur(2 * LANES, C_GLOW),
                  prev(w, C_RW_R), prev(w, C_RW_K), prev(w, C_RW_V), prev(LANES, C_WA), prev(2 * LANES, C_GLOW),
                  full((1, w)), full((1, w)), full((1, w)), full((1, LANES)), full((1, 2 * LANES)),
                  full((LANES, w)), full((LANES, w)), full((2 * LANES, w)),
                  full((1, w)), full((1, w)), full((1, w)), full((1, w)), full((1, w))],
        out_specs=[pl.BlockSpec((tm, w), lambda i: (i, 0))] * 8,
        out_shape=[out] * 8,
        compiler_params=_params(("parallel",), nbytes),
        name="rwkv_prep",
    )(y, y, y, y, y, y, y, y, y, y, mu_r, mu_k, mu_v, mu_wa, mu_g, wup, aup, gup, w0, a0, k_k, k_a, r_k)


RW_CHUNK = 64
RW_PACK = MXU_DIM // RW_N
RW_NEUMANN = 8
RW_STEP_CHUNKS = 4


def _rwkv_kernel(r_ref, lw_ref, k_ref, v_ref, kk_ref, a_ref, o_ref, s_ref):
    c, g4 = RW_CHUNK, MXU_DIM

    @pl.when(pl.program_id(0) == 0)
    def _():
        s_ref[...] = jnp.zeros_like(s_ref)

    ti = lax.broadcasted_iota(jnp.int32, (c, g4), 0)
    ci = lax.broadcasted_iota(jnp.int32, (c, g4), 1) % c
    strict = ci < ti
    incl = ci <= ti
    lane_head = lax.broadcasted_iota(jnp.int32, (1, g4), 1) // RW_N
    head_masks = [lane_head == h for h in range(RW_PACK)]
    bd_mask = (lax.broadcasted_iota(jnp.int32, (g4, g4), 0) // RW_N) == (lax.broadcasted_iota(jnp.int32, (g4, g4), 1) // RW_N)
    tr = lax.broadcasted_iota(jnp.int32, (RW_STEP_CHUNKS * c, RW_STEP_CHUNKS * c), 0)
    tc = lax.broadcasted_iota(jnp.int32, (RW_STEP_CHUNKS * c, RW_STEP_CHUNKS * c), 1)
    tri = jnp.where((tc <= tr) & (tc // c == tr // c), 1.0, 0.0).astype(BF16)

    def heads(b):
        p = b.astype(BF16)
        return jnp.concatenate([jnp.where(m, p, jnp.zeros_like(p)) for m in head_masks], axis=0)

    groups = range(BRANCH // g4)
    lanes = [slice(grp * g4, (grp + 1) * g4) for grp in groups]
    rows = [slice(n * c, (n + 1) * c) for n in range(RW_STEP_CHUNKS)]
    units = [(rw, sl) for rw in rows for sl in lanes]

    def each(fn, *cols):
        return [fn(*args) for args in zip(*cols)]

    def prod(a, b, dims=_NN):
        m = a.shape[0]
        both = lax.dot_general(jnp.concatenate(_split(a, 2), axis=0), b, dims, preferred_element_type=F32)
        return both[:m] + both[m:]

    lw_all = lw_ref[...]
    cum_all = _mm([tri], _split(lw_all, 3))
    lw = [lw_all[rw, sl] for rw, sl in units]
    cum = [cum_all[rw, sl] for rw, sl in units]
    cum_end = [x[c - 1:c, :] for x in cum]
    kk = [kk_ref[rw, sl] for rw, sl in units]
    kka = [x * a_ref[rw, sl] for x, (rw, sl) in zip(kk, units)]
    k = [k_ref[rw, sl] for rw, sl in units]
    v = [v_ref[rw, sl] for rw, sl in units]
    g_inv = each(lambda x: jnp.exp(-x), cum)
    tail = each(lambda e, x: jnp.exp(e - x), cum_end, cum)
    alp = each(lambda x, cm, l: x * jnp.exp(cm - l), kk, cum, lw)
    rho = each(lambda un, cm: r_ref[un[0], un[1]] * jnp.exp(cm), units, cum)
    ar = each(lambda x, y: jnp.concatenate([x, y], axis=0), alp, rho)
    pb = each(lambda x, y, gi: prod(x, heads(y * gi), _NT), ar, kka, g_inv)
    pk = each(lambda x, y, gi: prod(x, heads(y * gi), _NT), ar, k, g_inv)
    l_b = each(lambda p: jnp.where(strict, p[:c], 0.0), pb)
    m_b = each(lambda p: jnp.where(incl, p[c:], 0.0), pb)
    lm_k = each(lambda p: jnp.concatenate([jnp.where(strict, p[:c], 0.0), jnp.where(incl, p[c:], 0.0)], axis=0), pk)

    nb = RW_NEUMANN
    eye = jnp.where(ci == ti, 1.0, 0.0)
    l1 = each(lambda ll: jnp.where((ci // nb) == (ti // nb), ll, 0.0), l_b)
    l2 = each(lambda ll: prod(ll, heads(ll)), l1)
    l34 = each(lambda ll, sq: prod(jnp.concatenate([ll, sq], axis=0), heads(sq)), l1, l2)
    a1 = each(lambda ll, sq, p: eye - ll + sq - p[:c], l1, l2, l34)
    x = each(lambda aa, p: aa + prod(aa, heads(p[c:])), a1, l34)
    m = nb
    while m < c:
        lower = ((ti // m) % 2 == 1) & ((ci // m) == (ti // m) - 1)
        y = each(lambda xx, ll: prod(xx, heads(jnp.where(lower, ll, 0.0))), x, l_b)
        x = each(lambda xx, yy: xx - prod(yy, heads(xx)), x, y)
        m *= 2

    lmv = each(lambda lm, vv: prod(lm, heads(vv)), lm_k, v)
    kb = each(lambda kx, ka, t: jnp.concatenate([kx * t, ka * t], axis=0).astype(BF16), k, kka, tail)
    g_end = each(jnp.exp, cum_end)

    st = [s_ref[grp] for grp in groups]
    ng = len(lanes)
    for n, rw in enumerate(rows):
        mine = slice(n * ng, (n + 1) * ng)
        ars = each(lambda a, s: prod(a, s.astype(BF16), _NT), ar[mine], st)
        u = each(lambda xx, a, p: prod(xx, heads(a[:c] + p[:c])), x[mine], ars, lmv[mine])
        o = each(lambda a, p, mb, uu: a[c:] + p[c:] - prod(mb, heads(uu)), ars, lmv[mine], m_b[mine], u)
        vu_t = each(lambda vv, uu: jnp.concatenate([vv, -uu], axis=0).T, v[mine], u)
        st = each(lambda s, e, a, b: jnp.where(bd_mask, s * e + prod(a, b), 0.0), st, g_end[mine], vu_t, kb[mine])
        for grp in groups:
            o_ref[rw, lanes[grp]] = o[grp]
    for grp in groups:
        s_ref[grp] = st[grp]


def _rwkv_recurrence(r, lw, k, v, kk, a):
    s, w = r.shape
    c = RW_STEP_CHUNKS * RW_CHUNK
    spec = pl.BlockSpec((c, w), lambda i: (i, 0))
    nbytes = 2 * 7 * c * w * 4 + (w // MXU_DIM) * MXU_DIM * MXU_DIM * 4 + RW_STEP_CHUNKS * 64 * MXU_DIM * MXU_DIM * 4
    return pl.pallas_call(
        _rwkv_kernel,
        grid=(s // c,),
        in_specs=[spec] * 6,
        out_specs=spec,
        out_shape=jax.ShapeDtypeStruct((s, w), F32),
        scratch_shapes=[pltpu.VMEM((w // MXU_DIM, MXU_DIM, MXU_DIM), F32)],
        compiler_params=_params(("arbitrary",), nbytes),
        name="rwkv_recurrence",
    )(r, lw, k, v, kk, a)


def _rwkv_post_kernel(wkv_ref, g_ref, bonus_ref, lnw_ref, lnb_ref, o_ref):
    ones_bd = _head_ones()
    x = wkv_ref[...]
    d = x - _head_sum(x, ones_bd, 3) * (1.0 / RW_N)
    var = _head_sum(d * d, ones_bd, 3) * (1.0 / RW_N)
    y = d * lax.rsqrt(var + RW_LN_EPS) * lnw_ref[...] + lnb_ref[...]
    o_ref[...] = ((y + bonus_ref[...]) * g_ref[...]).astype(o_ref.dtype)


def _rwkv_post(wkv, g, bonus, ln_w, ln_b, tm=512):
    s, w = wkv.shape
    tm = min(tm, s)
    blk = pl.BlockSpec((tm, w), lambda i: (i, 0))
    vec = pl.BlockSpec((1, w), lambda i: (0, 0))
    nbytes = 2 * 4 * tm * w * 4 + 8 * tm * w * 4
    return pl.pallas_call(
        _rwkv_post_kernel,
        grid=(s // tm,),
        in_specs=[blk, blk, blk, vec, vec],
        out_specs=blk,
        out_shape=jax.ShapeDtypeStruct((s, w), BF16),
        compiler_params=_params(("parallel",), nbytes),
        name="rwkv_post",
    )(wkv, g, bonus, ln_w, ln_b)


def _merge_kernel(oa_ref, ob_ref, oc_ref, wb_ref, ga_ref, gb_ref, gc_ref, o_ref):
    acc = None
    for o_g, gate, idx in ((oa_ref, ga_ref, 0), (ob_ref, gb_ref, 1), (oc_ref, gc_ref, 2)):
        u = jnp.dot(o_g[...], wb_ref[idx], preferred_element_type=F32)
        t = jax.nn.sigmoid(gate[...]) * u
        acc = t if acc is None else acc + t
    o_ref[...] = acc.astype(o_ref.dtype)


def _merge(oa, ob, oc, w_branch, layer, y, tm=1024, tn=512):
    s, w = oa.shape
    d = w_branch.shape[3]
    tm = min(tm, s)
    branch = pl.BlockSpec((tm, w), lambda i, j: (i, 0))

    def gate(gi):
        return pl.BlockSpec((tm, tn), lambda i, j: (i, (C_GATES + gi * d) // tn + j))

    nbytes = 2 * 3 * tm * w * 2 + 2 * 3 * w * tn * 2 + 2 * 4 * tm * tn * 4 + 4 * tm * tn * 4
    return pl.pallas_call(
        _merge_kernel,
        grid=(s // tm, d // tn),
        in_specs=[branch, branch, branch, pl.BlockSpec((None, 3, w, tn), lambda i, j: (layer, 0, 0, j)),
                  gate(0), gate(1), gate(2)],
        out_specs=pl.BlockSpec((tm, tn), lambda i, j: (i, j)),
        out_shape=jax.ShapeDtypeStruct((s, d), BF16),
        compiler_params=_params(("parallel", "arbitrary"), nbytes),
        name="merge",
    )(oa, ob, oc, w_branch, y, y, y)


def _merge_out_kernel(oa_ref, ob_ref, oc_ref, wb_ref, ga_ref, gb_ref, gc_ref, wo_ref, x_ref, g_ref, o_ref, m_ref):
    j = pl.program_id(1)
    tn = ga_ref.shape[1]
    acc = None
    for o_g, gate, idx in ((oa_ref, ga_ref, 0), (ob_ref, gb_ref, 1), (oc_ref, gc_ref, 2)):
        u = jnp.dot(o_g[...], wb_ref[idx], preferred_element_type=F32)
        t = jax.nn.sigmoid(gate[...]) * u
        acc = t if acc is None else acc + t
    m_ref[:, pl.ds(pl.multiple_of(j * tn, tn), tn)] = acc.astype(m_ref.dtype)

    @pl.when(j == pl.num_programs(1) - 1)
    def _():
        mix = jnp.dot(m_ref[...], wo_ref[...], preferred_element_type=F32)
        o_ref[...] = x_ref[...] + _rms(mix, g_ref[...])


def _merge_out(oa, ob, oc, w_branch, w_out, layer, y, x, gain, tm=512, tn=512):
    s, w = oa.shape
    d = w_branch.shape[3]
    tm = min(tm, s)
    branch = pl.BlockSpec((tm, w), lambda i, j: (i, 0))
    rows = pl.BlockSpec((tm, d), lambda i, j: (i, 0))

    def gate(gi):
        return pl.BlockSpec((tm, tn), lambda i, j: (i, (C_GATES + gi * d) // tn + j))

    nbytes = (2 * 3 * tm * w * 2 + 2 * 3 * w * tn * 2 + 2 * 3 * tm * tn * 4 + 2 * d * d * 2 + 4 * tm * d * 4
              + tm * d * 2 + 2 * tm * d * 4)
    return pl.pallas_call(
        _merge_out_kernel,
        grid=(s // tm, d // tn),
        in_specs=[branch, branch, branch, pl.BlockSpec((None, 3, w, tn), lambda i, j: (layer, 0, 0, j)),
                  gate(0), gate(1), gate(2), pl.BlockSpec((None, d, d), lambda i, j: (layer, 0, 0)),
                  rows, pl.BlockSpec((1, d), lambda i, j: (0, 0))],
        out_specs=rows,
        out_shape=jax.ShapeDtypeStruct((s, d), F32),
        scratch_shapes=[pltpu.VMEM((tm, d), BF16)],
        compiler_params=_params(("parallel", "arbitrary"), nbytes),
        name="merge_out",
    )(oa, ob, oc, w_branch, y, y, y, w_out, x, gain)


def _out_proj_kernel(m_ref, w_ref, x_ref, g_ref, o_ref):
    mix = jnp.dot(m_ref[...], w_ref[...], preferred_element_type=F32)
    o_ref[...] = x_ref[...] + _rms(mix, g_ref[...])


def _out_proj(merged, w_out, layer, x, gain, tm=512):
    s, d = x.shape
    tm = min(tm, s)
    nbytes = 2 * tm * d * 2 + 2 * d * d * 2 + 4 * tm * d * 4 + 2 * tm * d * 4
    return pl.pallas_call(
        _out_proj_kernel,
        grid=(s // tm,),
        in_specs=[pl.BlockSpec((tm, d), lambda i: (i, 0)), pl.BlockSpec((None, d, d), lambda i: (layer, 0, 0)),
                  pl.BlockSpec((tm, d), lambda i: (i, 0)), pl.BlockSpec((1, d), lambda i: (0, 0))],
        out_specs=pl.BlockSpec((tm, d), lambda i: (i, 0)),
        out_shape=jax.ShapeDtypeStruct((s, d), F32),
        compiler_params=_params(("parallel",), nbytes),
        name="out_proj",
    )(merged, w_out, x, gain)


def _pad_cols(w, width):
    return jnp.pad(w, [(0, 0)] * (w.ndim - 1) + [(0, width - w.shape[-1])])


def _pad_rows(w, height, offset=0):
    return jnp.pad(w, ((offset, height - offset - w.shape[0]), (0, 0)))


def _pack_w_in(w):
    b = BRANCH
    gla0 = 3 * b
    alpha0 = gla0 + 2 * GLA_HEADS * GLA_DK + 2 * b
    rw0 = alpha0 + GLA_RANK
    wa0 = rw0 + 3 * b
    gl0 = wa0 + 2 * RW_RANK
    gates0 = gl0 + RW_GATE_RANK
    wt = jnp.swapaxes(w, 1, 2)

    def pad_to(t, rows):
        return jnp.pad(t, ((0, 0), (0, rows - t.shape[1]), (0, 0)))

    return jnp.concatenate([
        wt[:, :b] * (SB_HEAD_DIM ** -0.5), wt[:, b:3 * b],
        wt[:, gla0:alpha0], wt[:, rw0:wa0], wt[:, gates0:],
        pad_to(wt[:, alpha0:rw0], LANES), wt[:, wa0:gl0], pad_to(wt[:, gl0:gates0], 2 * LANES)],
        axis=1).astype(BF16)


def _token_mixing(x, gain, w_packed, layer, gla_gate_up, gla_gate_bias, gla_norm, rwkv_mu, rwkv_w_up, rwkv_w0,
                  rwkv_a_up, rwkv_a0, rwkv_g_up, rwkv_k_k, rwkv_k_a, rwkv_r_k, rwkv_ln_w, rwkv_ln_b,
                  w_branch, w_out, gain_post):
    b = BRANCH
    qkv = _norm_matmul(x, gain, w_packed, layer, 0, N_SB, BF16)
    y = _norm_matmul(x, gain, w_packed, layer, N_SB, N_REST, F32)

    o_a = _sb_attention(qkv)

    o_b = _gla(y, _pad_rows(gla_gate_up, LANES).astype(BF16), gla_gate_bias[None, :], gla_norm[None, :])

    mu = rwkv_mu
    row = lambda t: t[None, :]
    prep = _rwkv_prep(
        y, row(mu[:b]), row(mu[b:2 * b]), row(mu[2 * b:3 * b]), row(mu[3 * b:3 * b + 2 * RW_RANK]),
        _pad_cols(row(mu[3 * b + 2 * RW_RANK:]), 2 * LANES),
        _pad_rows(rwkv_w_up, LANES).astype(BF16), _pad_rows(rwkv_a_up, LANES, RW_RANK).astype(BF16),
        _pad_rows(rwkv_g_up, 2 * LANES).astype(BF16),
        row(rwkv_w0), row(rwkv_a0), row(rwkv_k_k), row(rwkv_k_a), row(rwkv_r_k.reshape(-1)))
    r, lw, k, v, kk, a, g, bonus = prep
    wkv = _rwkv_recurrence(r, lw, k, v, kk, a)
    o_c = _rwkv_post(wkv, g, bonus, row(rwkv_ln_w), row(rwkv_ln_b))

    return _merge_out(o_a, o_b, o_c, w_branch, w_out, layer, y, x, gain_post)


def kernel(x, norm_pre, norm_post, ffn_in, ffn_out, w_in, gla_gate_up, gla_gate_bias, gla_norm, rwkv_mu,
           rwkv_w_up, rwkv_w0, rwkv_a_up, rwkv_a0, rwkv_g_up, rwkv_k_k, rwkv_k_a, rwkv_r_k, rwkv_ln_w,
           rwkv_ln_b, w_branch, w_out):
    batch, seq, d = x.shape
    assert batch == 1 and d == D_MODEL
    h = x[0]
    w_packed = _pack_w_in(w_in)
    w_branch, w_out = w_branch.astype(BF16), w_out.astype(BF16)
    for l in range(norm_pre.shape[0]):
        h = _ffn(h, norm_pre[l, 0][None, :], norm_post[l, 0][None, :], ffn_in, ffn_out, l, 0)
        h = _token_mixing(h, norm_pre[l, 1][None, :], w_packed, l, gla_gate_up[l], gla_gate_bias[l], gla_norm[l],
                          rwkv_mu[l], rwkv_w_up[l], rwkv_w0[l], rwkv_a_up[l], rwkv_a0[l], rwkv_g_up[l],
                          rwkv_k_k[l], rwkv_k_a[l], rwkv_r_k[l], rwkv_ln_w[l], rwkv_ln_b[l],
                          w_branch, w_out, norm_post[l, 1][None, :])
        h = _ffn(h, norm_pre[l, 2][None, :], norm_post[l, 2][None, :], ffn_in, ffn_out, l, 1)
    return h[None]
```
